```python
import math
import jax, jax.numpy as jnp
from jax import lax
import numpy as np

D_MODEL = 1024
BATCH = 16
SEQ = 2048
DEPTH = 1
DEC_BATCH = 128
DEC_SEQ = 8
PAST_LEN = 8192
PAGE_SIZE = 128

N_META = 16
ROPE_THETA = 10000.0
RMS_EPS = 1e-6
NEG_INF = -1e30
Q_BLOCK = 128

MLA_HEADS = 8
MLA_NOPE = 64
MLA_ROPE = 32
MLA_QK = MLA_NOPE + MLA_ROPE
MLA_V = 64
Q_LORA = 256
KV_LORA = 128

DIFF_HEADS = 4
DIFF_HD = 64
DIFF_V = 2 * DIFF_HD
DIFF_QK_COLS = 2 * DIFF_HEADS * DIFF_HD

OFF_CKV = Q_LORA
OFF_KR = OFF_CKV + KV_LORA
OFF_DQ = OFF_KR + MLA_ROPE
OFF_DK = OFF_DQ + DIFF_QK_COLS
OFF_DV = OFF_DK + DIFF_QK_COLS
MIX_COLS = OFF_DV + DIFF_HEADS * DIFF_V
MIX_WIDTH = MLA_HEADS * MLA_V + DIFF_HEADS * DIFF_V

N_GROUPS = 4
EXP_PER_GROUP = 8
TOP_K_IN_GROUP = 2
D_EXPERT = 256

kernel_name = 'hymba_mla_diffattn_hmoe_step'


def rmsnorm(x, g):
    xf = x.astype(jnp.float32)
    xf = xf * lax.rsqrt(jnp.mean(xf * xf, axis=-1, keepdims=True) + RMS_EPS)
    return (xf * g.astype(jnp.float32)).astype(x.dtype)


def rope(x, pos):
    d = x.shape[-1]
    inv = 1.0 / (ROPE_THETA ** (jnp.arange(0, d, 2, dtype=jnp.float32) / d))
    ang = pos.astype(jnp.float32)[:, None] * inv[None, :]
    cos = jnp.cos(ang)[:, None, :]
    sin = jnp.sin(ang)[:, None, :]
    xf = x.astype(jnp.float32)
    x1, x2 = xf[..., : d // 2], xf[..., d // 2:]
    return jnp.concatenate([x1 * cos - x2 * sin, x2 * cos + x1 * sin], axis=-1).astype(x.dtype)


def rope_tail(x, pos):
    return jnp.concatenate([x[..., :MLA_NOPE], rope(x[..., MLA_NOPE:], pos)], axis=-1)


def mix_project(u, pos, w_in, g_cq, g_ckv, w_uq, g_qn_mla, g_qn_diff, g_kn_diff):
    B, L, _ = u.shape
    proj = u @ w_in
    cq, ckv, kr, dq, dk, dv = jnp.split(proj, [OFF_CKV, OFF_KR, OFF_DQ, OFF_DK, OFF_DV], axis=-1)
    cq = rmsnorm(cq, g_cq)
    ckv = rmsnorm(ckv, g_ckv)
    q_mla = (cq @ w_uq).reshape(B, L, MLA_HEADS, MLA_QK)
    q_mla = rope_tail(rmsnorm(q_mla, g_qn_mla), pos)
    dq = rope(rmsnorm(dq.reshape(B, L, 2 * DIFF_HEADS, DIFF_HD), g_qn_diff), pos)
    dk = rope(rmsnorm(dk.reshape(B, L, 2 * DIFF_HEADS, DIFF_HD), g_kn_diff), pos)
    dv = dv.reshape(B, L, DIFF_HEADS, DIFF_V)
    return q_mla, ckv, kr, dq, dk, dv


def mla_keys(ckv, kr, pos, w_ukv, g_kn_mla):
    B, L, _ = ckv.shape
    kv = (ckv @ w_ukv).reshape(B, L, MLA_HEADS, MLA_NOPE + MLA_V)
    k_nope, v = kv[..., :MLA_NOPE], kv[..., MLA_NOPE:]
    k = jnp.concatenate([k_nope, jnp.broadcast_to(kr[:, :, None, :], (B, L, MLA_HEADS, MLA_ROPE))], axis=-1)
    k = rope_tail(rmsnorm(k, g_kn_mla), pos)
    return k, v


def causal_probs(q, k, qpos, kpos):
    s = jnp.einsum('bqhd,bkhd->bhqk', q, k, preferred_element_type=jnp.float32)
    s = s * (q.shape[-1] ** -0.5)
    s = jnp.where(kpos[None, :] <= qpos[:, None], s, NEG_INF)
    return jax.nn.softmax(s, axis=-1)


def mix_attend(q_mla, k_mla, v_mla, dq, dk, dv, qpos, kpos, lam, lam_init, g_o_mla, g_subln):
    B, Lq = q_mla.shape[0], q_mla.shape[1]
    p = causal_probs(q_mla, k_mla, qpos, kpos)
    o_a = jnp.einsum('bhqk,bkhd->bqhd', p.astype(v_mla.dtype), v_mla)
    o_a = rmsnorm(o_a, g_o_mla).reshape(B, Lq, MLA_HEADS * MLA_V)
    pd = causal_probs(dq, dk, qpos, kpos)
    pd = pd.reshape(B, DIFF_HEADS, 2, Lq, pd.shape[-1])
    a = pd[:, :, 0] - lam * pd[:, :, 1]
    o_b = jnp.einsum('bhqk,bkhd->bqhd', a.astype(dv.dtype), dv)
    o_b = (rmsnorm(o_b, g_subln) * (1.0 - lam_init)).reshape(B, Lq, DIFF_HEADS * DIFF_V)
    return jnp.concatenate([o_a, o_b], axis=-1)


def hier_moe(h, w_gate_group, b_gate_group, w_gate_expert, b_gate_expert, w1, w3, w2):
    gl = (h @ w_gate_group).astype(jnp.float32) + b_gate_group.astype(jnp.float32)
    gp = jax.nn.softmax(gl, axis=-1)
    gsel = jnp.argmax(gl, axis=-1)
    g_onehot = jax.nn.one_hot(gsel, N_GROUPS, dtype=jnp.float32)
    gw = jnp.sum(gp * g_onehot, axis=-1)
    el = jnp.einsum('td,dge->tge', h, w_gate_expert).astype(jnp.float32) + b_gate_expert.astype(jnp.float32)
    el_sel = jnp.take_along_axis(el, gsel[:, None, None], axis=1)[:, 0]
    top_v, top_i = lax.top_k(el_sel, TOP_K_IN_GROUP)
    top_w = jax.nn.softmax(top_v, axis=-1)
    ew = jnp.sum(jax.nn.one_hot(top_i, EXP_PER_GROUP, dtype=jnp.float32) * top_w[..., None], axis=1)
    comb = (gw[:, None, None] * g_onehot[:, :, None] * ew[:, None, :]).astype(h.dtype)
    out = jnp.zeros_like(h)
    for g in range(N_GROUPS):
        a = jnp.einsum('td,edf->tef', h, w1[g])
        b = jnp.einsum('td,edf->tef', h, w3[g])
        hid = jax.nn.silu(a) * b * comb[:, g, :, None]
        out = out + jnp.einsum('tef,efd->td', hid, w2[g])
    return out


def setup_inputs(seed: int = 0) -> dict:
    key = jax.random.key(seed)
    ks = list(jax.random.split(key, 48))
    f32 = jnp.float32
    n_pages = PAST_LEN // PAGE_SIZE
    n_used = DEC_BATCH * n_pages
    n_pool = n_used + max(1, n_used // 4)

    def nrm(shape, scale=1.0):
        return jax.random.normal(ks.pop(), shape, f32) * scale

    def gain(shape):
        return 1.0 + 0.01 * jax.random.normal(ks.pop(), shape, f32)

    perm = jax.random.permutation(ks.pop(), n_pool)
    page_table = perm[:n_used].reshape(DEC_BATCH, n_pages).astype(jnp.int32)
    return {
        'x_prompt': nrm((BATCH, SEQ, D_MODEL)),
        'x_sample': nrm((DEC_BATCH, DEC_SEQ, D_MODEL)),
        'cache_ckv': nrm((DEPTH, n_pool, PAGE_SIZE, KV_LORA)),
        'cache_krope': nrm((DEPTH, n_pool, PAGE_SIZE, MLA_ROPE)),
        'cache_dk': nrm((DEPTH, n_pool, PAGE_SIZE, 2 * DIFF_HEADS, DIFF_HD)),
        'cache_dv': nrm((DEPTH, n_pool, PAGE_SIZE, DIFF_HEADS, DIFF_V)),
        'page_table': page_table,
        'meta_tokens': nrm((N_META, D_MODEL)),
        'g_attn': gain((DEPTH, D_MODEL)),
        'w_in': nrm((DEPTH, D_MODEL, MIX_COLS), D_MODEL ** -0.5),
        'g_cq': gain((DEPTH, Q_LORA)),
        'g_ckv': gain((DEPTH, KV_LORA)),
        'w_uq': nrm((DEPTH, Q_LORA, MLA_HEADS * MLA_QK), Q_LORA ** -0.5),
        'g_qn_mla': gain((DEPTH, MLA_QK)),
        'w_ukv': nrm((DEPTH, KV_LORA, MLA_HEADS * (MLA_NOPE + MLA_V)), KV_LORA ** -0.5),
        'g_kn_mla': gain((DEPTH, MLA_QK)),
        'g_o_mla': gain((DEPTH, MLA_V)),
        'g_qn_diff': gain((DEPTH, DIFF_HD)),
        'g_kn_diff': gain((DEPTH, DIFF_HD)),
        'lambda_q1': nrm((DEPTH, DIFF_HD), 0.1),
        'lambda_k1': nrm((DEPTH, DIFF_HD), 0.1),
        'lambda_q2': nrm((DEPTH, DIFF_HD), 0.1),
        'lambda_k2': nrm((DEPTH, DIFF_HD), 0.1),
        'g_subln': gain((DEPTH, DIFF_V)),
        'w_o': nrm((DEPTH, MIX_WIDTH, D_MODEL), MIX_WIDTH ** -0.5),
        'g_ffn': gain((DEPTH, D_MODEL)),
        'w_gate_group': nrm((DEPTH, D_MODEL, N_GROUPS), D_MODEL ** -0.5),
        'b_gate_group': nrm((DEPTH, N_GROUPS), 0.01),
        'w_gate_expert': nrm((DEPTH, D_MODEL, N_GROUPS, EXP_PER_GROUP), D_MODEL ** -0.5),
        'b_gate_expert': nrm((DEPTH, N_GROUPS, EXP_PER_GROUP), 0.01),
        'w1': nrm((DEPTH, N_GROUPS, EXP_PER_GROUP, D_MODEL, D_EXPERT), D_MODEL ** -0.5),
        'w3': nrm((DEPTH, N_GROUPS, EXP_PER_GROUP, D_MODEL, D_EXPERT), D_MODEL ** -0.5),
        'w2': nrm((DEPTH, N_GROUPS, EXP_PER_GROUP, D_EXPERT, D_MODEL), D_EXPERT ** -0.5),
    }


def reference(x_prompt, x_sample, cache_ckv, cache_krope, cache_dk, cache_dv, page_table,
              meta_tokens, g_attn, w_in, g_cq, g_ckv, w_uq, g_qn_mla, w_ukv, g_kn_mla,
              g_o_mla, g_qn_diff, g_kn_diff, lambda_q1, lambda_k1, lambda_q2, lambda_k2,
              g_subln, w_o, g_ffn, w_gate_group, b_gate_group, w_gate_expert, b_gate_expert,
              w1, w3, w2):
    B = x_prompt.shape[0]
    meta = jnp.broadcast_to(meta_tokens[None].astype(x_prompt.dtype), (B, N_META, D_MODEL))
    xp = jnp.concatenate([meta, x_prompt], axis=1)
    L = xp.shape[1]
    nb = -(-L // Q_BLOCK)
    Lp = nb * Q_BLOCK
    xs = x_sample
    Bs, S = xs.shape[0], xs.shape[1]
    past_len = page_table.shape[1] * PAGE_SIZE

    st = {n: [] for n in ('ckv_p', 'kr_p', 'dk_p', 'dv_p', 'ckv_s', 'kr_s', 'dk_s', 'dv_s')}
    for l in range(DEPTH):
        lam_init = 0.8 - 0.6 * math.exp(-0.3 * l)
        lam = (jnp.exp(jnp.sum(lambda_q1[l].astype(jnp.float32) * lambda_k1[l].astype(jnp.float32)))
               - jnp.exp(jnp.sum(lambda_q2[l].astype(jnp.float32) * lambda_k2[l].astype(jnp.float32)))
               + lam_init)
        proj_w = (w_in[l], g_cq[l], g_ckv[l], w_uq[l], g_qn_mla[l], g_qn_diff[l], g_kn_diff[l])
        moe_w = (w_gate_group[l], b_gate_group[l], w_gate_expert[l], b_gate_expert[l], w1[l], w3[l], w2[l])

        u = jnp.pad(rmsnorm(xp, g_attn[l]), ((0, 0), (0, Lp - L), (0, 0)))
        pos = jnp.arange(Lp)
        q_mla, ckv, kr, dq, dk, dv = mix_project(u, pos, *proj_w)
        k_mla, v_mla = mla_keys(ckv, kr, pos, w_ukv[l], g_kn_mla[l])

        def q_block(i):
            s0 = i * Q_BLOCK
            qm = lax.dynamic_slice_in_dim(q_mla, s0, Q_BLOCK, axis=1)
            qd = lax.dynamic_slice_in_dim(dq, s0, Q_BLOCK, axis=1)
            return mix_attend(qm, k_mla, v_mla, qd, dk, dv, s0 + jnp.arange(Q_BLOCK), pos,
                              lam, lam_init, g_o_mla[l], g_subln[l])

        o = lax.map(q_block, jnp.arange(nb))
        o = jnp.moveaxis(o, 0, 1).reshape(B, Lp, MIX_WIDTH)[:, :L]
        hp = xp + o @ w_o[l]
        xp = hp + hier_moe(rmsnorm(hp, g_ffn[l]).reshape(B * L, D_MODEL), *moe_w).reshape(B, L, D_MODEL)
        st['ckv_p'].append(ckv[:, :L]); st['kr_p'].append(kr[:, :L])
        st['dk_p'].append(dk[:, :L]); st['dv_p'].append(dv[:, :L])

        us = rmsnorm(xs, g_attn[l])
        qpos = past_len + jnp.arange(S)
        kpos = jnp.arange(past_len + S)
        sq_mla, sckv, skr, sdq, sdk, sdv = mix_project(us, qpos, *proj_w)
        pool_ckv, pool_kr, pool_dk, pool_dv = cache_ckv[l], cache_krope[l], cache_dk[l], cache_dv[l]

        def one_seq(args):
            pages, qm, ckv_n, kr_n, qd, dk_n, dv_n = args

            def gather(pool, new):
                past = pool[pages].reshape((past_len,) + pool.shape[2:])
                return jnp.concatenate([past, new], axis=0)[None]

            k_a, v_a = mla_keys(gather(pool_ckv, ckv_n), gather(pool_kr, kr_n), kpos, w_ukv[l], g_kn_mla[l])
            out = mix_attend(qm[None], k_a, v_a, qd[None], gather(pool_dk, dk_n), gather(pool_dv, dv_n),
                             qpos, kpos, lam, lam_init, g_o_mla[l], g_subln[l])
            return out[0]

        os_ = lax.map(one_seq, (page_table, sq_mla, sckv, skr, sdq, sdk, sdv))
        hs = xs + os_ @ w_o[l]
        xs = hs + hier_moe(rmsnorm(hs, g_ffn[l]).reshape(Bs * S, D_MODEL), *moe_w).reshape(Bs, S, D_MODEL)
        st['ckv_s'].append(sckv); st['kr_s'].append(skr)
        st['dk_s'].append(sdk); st['dv_s'].append(sdv)

    y_prompt = xp[:, N_META:]
    y_sample = xs
    ckv_prompt = jnp.stack(st['ckv_p'])
    krope_prompt = jnp.stack(st['kr_p'])
    dk_prompt = jnp.stack(st['dk_p'])
    dv_prompt = jnp.stack(st['dv_p'])
    ckv_sample = jnp.stack(st['ckv_s'])
    krope_sample = jnp.stack(st['kr_s'])
    dk_sample = jnp.stack(st['dk_s'])
    dv_sample = jnp.stack(st['dv_s'])
    return (y_prompt, y_sample, ckv_prompt, krope_prompt, dk_prompt, dv_prompt,
            ckv_sample, krope_sample, dk_sample, dv_sample)
```

```python
import functools
import math

import jax
import jax.numpy as jnp
from jax import lax
from jax.experimental import pallas as pl
from jax.experimental.pallas import tpu as pltpu

F32 = jnp.float32
BF16 = jnp.bfloat16
I32 = jnp.int32

RMS_EPS = 1e-6
ROPE_THETA = 10000.0
NEG_INF = -1e30
PAGE = 128
LANES = 128

MLA_HEADS = 8
MLA_NOPE = 64
MLA_ROPE = 32
MLA_QK = MLA_NOPE + MLA_ROPE
MLA_V = 64
Q_LORA = 256
KV_LORA = 128
DIFF_HEADS = 4
DIFF_HD = 64
DIFF_V = 128
DIFF_MAPS = 2 * DIFF_HEADS
N_GROUPS = 4
EXP_PER_GROUP = 8
D_EXPERT = 256

VMEM_LIMIT = 48 * 1024 * 1024

N_TAB = 12


def _dot(a, b):
    return jnp.dot(a, b, preferred_element_type=F32)


def _dot_nt(a, b):
    return lax.dot_general(a, b, (((1,), (1,)), ((), ())), preferred_element_type=F32)


def _rope_cs(pos, d):
    inv = 1.0 / (ROPE_THETA ** (jnp.arange(0, d, 2, dtype=F32) / d))
    ang = pos.astype(F32)[:, None] * inv[None, :]
    return jnp.cos(ang), jnp.sin(ang)


def _mla_tables(pos, g, scale):
    c, s = _rope_cs(pos, MLA_ROPE)
    n = pos.shape[0]
    h = MLA_ROPE // 2
    gn, g1, g2 = g[:MLA_NOPE], g[MLA_NOPE:MLA_NOPE + h], g[MLA_NOPE + h:]
    z = lambda w: jnp.zeros((n, w), F32)
    big_c = jnp.concatenate([jnp.broadcast_to(gn[None], (n, MLA_NOPE)), c * g1, c * g2, z(LANES - MLA_QK)], 1)
    big_a = jnp.concatenate([z(MLA_NOPE), -s * g2, z(LANES - MLA_NOPE - h)], 1)
    big_b = jnp.concatenate([z(MLA_NOPE + h), s * g1, z(LANES - MLA_QK)], 1)
    return [big_c * scale, big_a * scale, big_b * scale]


def _diff_tables(pos, g, scale):
    c, s = _rope_cs(pos, DIFF_HD)
    n = pos.shape[0]
    h = DIFF_HD // 2
    z = jnp.zeros((n, h), F32)
    c64 = jnp.concatenate([c * g[:h], c * g[h:]], 1)
    a64 = jnp.concatenate([-s * g[h:], z], 1)
    b64 = jnp.concatenate([z, s * g[:h]], 1)
    return [jnp.tile(t, (1, 2)) * scale for t in (c64, a64, b64)]


def _tables(pos, g_qn_mla, g_kn_mla, g_qn_diff, g_kn_diff):
    log2e = math.log2(math.e)
    t = (_mla_tables(pos, g_qn_mla, MLA_QK ** -0.5 * log2e) + _mla_tables(pos, g_kn_mla, 1.0)
         + _diff_tables(pos, g_qn_diff, DIFF_HD ** -0.5 * log2e) + _diff_tables(pos, g_kn_diff, 1.0))
    return jnp.concatenate(t, axis=1)


def _prep_proj_weights(w_in, w_uq, w_ukv):
    d = w_in.shape[0]
    off_ckv, off_kr = Q_LORA, Q_LORA + KV_LORA
    off_rest = off_kr + MLA_ROPE
    kr = w_in[:, off_kr:off_rest]
    z = jnp.zeros((d, MLA_ROPE), w_in.dtype)
    w_in_p = jnp.concatenate([w_in[:, :off_kr], kr, z, kr, z, w_in[:, off_rest:]], axis=1).astype(BF16)
    wq = w_uq.reshape(Q_LORA, MLA_HEADS, MLA_QK)
    wq = jnp.pad(wq, ((0, 0), (0, 0), (0, LANES - MLA_QK))).reshape(Q_LORA, MLA_HEADS * LANES).astype(BF16)
    wkv = w_ukv.reshape(KV_LORA, MLA_HEADS, MLA_NOPE + MLA_V)
    wk = jnp.pad(wkv[..., :MLA_NOPE], ((0, 0), (0, 0), (0, LANES - MLA_NOPE))).reshape(KV_LORA, MLA_HEADS * LANES)
    wv = wkv[..., MLA_NOPE:].reshape(KV_LORA, MLA_HEADS * MLA_V)
    wkv_p = jnp.concatenate([wk, wv], axis=1).astype(BF16)
    return w_in_p, wq, wkv_p


def _seg_ones(width):
    i = jnp.arange(LANES)
    return (i[:, None] // width == i[None, :] // width).astype(BF16)


def _segsum(sq, e):
    hi = sq.astype(BF16)
    lo = (sq - hi.astype(F32)).astype(BF16)
    return _dot(hi, e) + _dot(lo, e)


def _norm_rope(x, tab_ref, t0, e, inv_n, sh_up, sh_dn):
    r = lax.rsqrt(_segsum(x * x, e) * inv_n + RMS_EPS)
    c = tab_ref[:, (t0 + 0) * LANES:(t0 + 1) * LANES]
    a = tab_ref[:, (t0 + 1) * LANES:(t0 + 2) * LANES]
    b = tab_ref[:, (t0 + 2) * LANES:(t0 + 3) * LANES]
    rot = x * c + pltpu.roll(x, sh_up, 1) * a + pltpu.roll(x, sh_dn, 1) * b
    return rot * r


def _proj_kernel(x_ref, tab_ref, gattn_ref, win_ref, gcq_ref, gckv_ref, wuq_ref, wkv_ref, e128_ref, e64_ref,
                 ckv_ref, kr_ref, dk32_ref, dv32_ref, qm_ref, km_ref, vm_ref, dq_ref, dk16_ref, dv16_ref):
    x = x_ref[...]
    u = x * lax.rsqrt(jnp.mean(x * x, axis=-1, keepdims=True) + RMS_EPS) * gattn_ref[...]
    proj = _dot(u.astype(BF16), win_ref[...])
    o_ckv, o_kr, o_dq = Q_LORA, Q_LORA + KV_LORA, Q_LORA + KV_LORA + LANES
    o_dk = o_dq + DIFF_MAPS * DIFF_HD
    o_dv = o_dk + DIFF_MAPS * DIFF_HD

    cq = proj[:, :Q_LORA]
    cqn = cq * lax.rsqrt(jnp.mean(cq * cq, axis=-1, keepdims=True) + RMS_EPS) * gcq_ref[...]
    ckv = proj[:, o_ckv:o_kr]
    ckvn = ckv * lax.rsqrt(jnp.mean(ckv * ckv, axis=-1, keepdims=True) + RMS_EPS) * gckv_ref[...]
    ckv_ref[...] = ckvn
    krs = proj[:, o_kr:o_dq]
    kr_ref[...] = krs[:, :MLA_ROPE]

    e128 = e128_ref[...]
    e64 = e64_ref[...]
    up, dn = LANES - MLA_ROPE // 2, MLA_ROPE // 2
    qm = _dot(cqn.astype(BF16), wuq_ref[...])
    for h in range(MLA_HEADS):
        sl = slice(h * LANES, (h + 1) * LANES)
        qm_ref[:, sl] = _norm_rope(qm[:, sl], tab_ref, 0, e128, 1.0 / MLA_QK, up, dn).astype(BF16)

    kv = _dot(ckvn.astype(BF16), wkv_ref[...])
    lane = lax.broadcasted_iota(I32, (1, LANES), 1)
    kr_mid = krs * ((lane >= MLA_NOPE) & (lane < MLA_QK)).astype(F32)
    for h in range(MLA_HEADS):
        sl = slice(h * LANES, (h + 1) * LANES)
        km_ref[:, sl] = _norm_rope(kv[:, sl] + kr_mid, tab_ref, 3, e128, 1.0 / MLA_QK, up, dn).astype(BF16)
    vm_ref[...] = kv[:, MLA_HEADS * LANES:].astype(BF16)

    up, dn = LANES - DIFF_HD // 2, DIFF_HD // 2
    for j in range(DIFF_MAPS * DIFF_HD // LANES):
        sl = slice(j * LANES, (j + 1) * LANES)
        dq_ref[:, sl] = _norm_rope(proj[:, o_dq + j * LANES:o_dq + (j + 1) * LANES], tab_ref, 6, e64,
                                   1.0 / DIFF_HD, up, dn).astype(BF16)
        dk = _norm_rope(proj[:, o_dk + j * LANES:o_dk + (j + 1) * LANES], tab_ref, 9, e64, 1.0 / DIFF_HD, up, dn)
        dk32_ref[:, sl] = dk
        dk16_ref[:, sl] = dk.astype(BF16)
    dv = proj[:, o_dv:]
    dv32_ref[...] = dv
    dv16_ref[...] = dv.astype(BF16)


def _project(x, tab, n_batch, tm, consts):
    t, d = x.shape
    n_i = t // n_batch // tm
    row = lambda b, i: (b * n_i + i, 0)
    fixed = lambda b, i: (0, 0)
    gattn, win, gcq, gckv, wuq, wkv, e128, e64 = consts
    widths = [(KV_LORA, F32), (MLA_ROPE, F32), (DIFF_MAPS * DIFF_HD, F32), (DIFF_HEADS * DIFF_V, F32),
              (MLA_HEADS * LANES, BF16), (MLA_HEADS * LANES, BF16), (MLA_HEADS * MLA_V, BF16),
              (DIFF_MAPS * DIFF_HD, BF16), (DIFF_MAPS * DIFF_HD, BF16), (DIFF_HEADS * DIFF_V, BF16)]
    return pl.pallas_call(
        _proj_kernel,
        grid=(n_batch, n_i),
        in_specs=[pl.BlockSpec((tm, d), row), pl.BlockSpec((tm, N_TAB * LANES), lambda b, i: (i, 0))]
        + [pl.BlockSpec(c.shape, fixed) for c in consts],
        out_specs=[pl.BlockSpec((tm, w), row) for w, _ in widths],
        out_shape=[jax.ShapeDtypeStruct((t, w), dt) for w, dt in widths],
        compiler_params=pltpu.CompilerParams(dimension_semantics=("arbitrary", "arbitrary"),
                                             vmem_limit_bytes=VMEM_LIMIT),
        name="project",
    )(x, tab, *consts)


def _lane_tile(x, width):
    if width <= LANES:
        return x[:, :width]
    return jnp.concatenate([x] * (width // LANES), axis=1)


def _flash_update(a, q, k, v, m_scr, l_scr, acc_scr):
    s = _dot_nt(q, k)
    m_prev = m_scr[a]
    m_next = jnp.maximum(m_prev, jnp.max(s, axis=1, keepdims=True))
    alpha = jnp.exp2(m_prev - m_next)
    p = jnp.exp2(s - _lane_tile(m_next, s.shape[1]))
    l_scr[a] = alpha * l_scr[a] + jnp.sum(p, axis=1, keepdims=True)
    acc_scr[a] = acc_scr[a] * alpha + _dot(p.astype(BF16), v)
    m_scr[a] = m_next


def _pattn_kernel(q_ref, k_ref, v_ref, kmeta_ref, vmeta_ref, gain_ref, lam_ref, o_ref, m_scr, l_scr, acc_scr,
                  *, mla, tq, n_meta, post_scale):
    i = pl.program_id(2)
    lane = lax.broadcasted_iota(I32, (1, LANES), 1)
    lo_half = lane < LANES // 2
    if mla:
        qs = [q_ref[:, :LANES], q_ref[:, LANES:]]
        ksl = [slice(0, LANES), slice(LANES, 2 * LANES)]
    else:
        q = q_ref[...]
        qs = [q * lo_half.astype(BF16), q * (1 - lo_half.astype(I32)).astype(BF16)]
        ksl = [slice(0, LANES), slice(0, LANES)]

    k0 = pl.multiple_of(i * tq, tq)
    causal = lax.broadcasted_iota(I32, (tq, tq), 0) >= lax.broadcasted_iota(I32, (tq, tq), 1)
    meta_ok = lane < n_meta
    v_diag = v_ref[pl.ds(k0, tq), :]
    v_meta = vmeta_ref[...]
    for a in range(2):
        s1 = jnp.where(causal, _dot_nt(qs[a], k_ref[pl.ds(k0, tq), ksl[a]]), NEG_INF)
        s2 = jnp.where(meta_ok, _dot_nt(qs[a], kmeta_ref[:, ksl[a]]), NEG_INF)
        m = jnp.maximum(jnp.max(s1, axis=1, keepdims=True), jnp.max(s2, axis=1, keepdims=True))
        p1 = jnp.exp2(s1 - m)
        p2 = jnp.exp2(s2 - m)
        m_scr[a] = jnp.broadcast_to(m, (tq, LANES))
        l_scr[a] = jnp.broadcast_to(jnp.sum(p1, axis=1, keepdims=True) + jnp.sum(p2, axis=1, keepdims=True),
                                    (tq, LANES))
        acc_scr[a] = _dot(p1.astype(BF16), v_diag) + _dot(p2.astype(BF16), v_meta)

    def body(j, carry):
        kj = pl.multiple_of(j * tq, tq)
        v = v_ref[pl.ds(kj, tq), :]
        for a in range(2):
            _flash_update(a, qs[a], k_ref[pl.ds(kj, tq), ksl[a]], v, m_scr, l_scr, acc_scr)
        return carry

    lax.fori_loop(0, i, body, 0)

    o0 = acc_scr[0] / l_scr[0]
    o1 = acc_scr[1] / l_scr[1]
    if mla:
        o = jnp.where(lo_half, o0, o1)
        sq = o * o
        ss_lo = jnp.sum(jnp.where(lo_half, sq, 0.0), axis=1, keepdims=True)
        ss_hi = jnp.sum(jnp.where(lo_half, 0.0, sq), axis=1, keepdims=True)
        r = jnp.where(lo_half, lax.rsqrt(ss_lo * (2.0 / LANES) + RMS_EPS), lax.rsqrt(ss_hi * (2.0 / LANES) + RMS_EPS))
    else:
        o = o0 - _lambda(lam_ref) * o1
        r = lax.rsqrt(jnp.mean(o * o, axis=1, keepdims=True) + RMS_EPS)
    o_ref[...] = (o * r * gain_ref[...] * post_scale).astype(o_ref.dtype)


def _lambda(lam_ref):
    lam = lam_ref[...]
    a = jnp.exp(jnp.sum(lam[0:1] * lam[1:2], axis=1, keepdims=True))
    b = jnp.exp(jnp.sum(lam[2:3] * lam[3:4], axis=1, keepdims=True))
    return a - b + lam[4:5, 0:1]


def _prompt_attn(q, k, v, kmeta, vmeta, gain, lam, n_batch, tq, mla, post_scale):
    t = q.shape[0]
    s = t // n_batch
    n_q = s // tq
    n_meta = kmeta.shape[0]
    assert n_meta <= LANES
    kmeta = jnp.pad(kmeta, ((0, LANES - n_meta), (0, 0)))
    vmeta = jnp.pad(vmeta, ((0, LANES - n_meta), (0, 0)))
    kw = 2 * LANES if mla else LANES
    n_hp = q.shape[1] // kw
    kern = functools.partial(_pattn_kernel, mla=mla, tq=tq, n_meta=n_meta, post_scale=post_scale)
    return pl.pallas_call(
        kern,
        grid=(n_batch, n_hp, n_q),
        in_specs=[pl.BlockSpec((tq, kw), lambda b, h, i: (b * n_q + i, h)),
                  pl.BlockSpec((s, kw), lambda b, h, i: (b, h)),
                  pl.BlockSpec((s, LANES), lambda b, h, i: (b, h)),
                  pl.BlockSpec((LANES, kw), lambda b, h, i: (0, h)),
                  pl.BlockSpec((LANES, LANES), lambda b, h, i: (0, h)),
                  pl.BlockSpec((1, LANES), lambda b, h, i: (0, 0)),
                  pl.BlockSpec(lam.shape, lambda b, h, i: (0, 0))],
        out_specs=pl.BlockSpec((tq, LANES), lambda b, h, i: (b * n_q + i, h)),
        out_shape=jax.ShapeDtypeStruct((t, n_hp * LANES), BF16),
        scratch_shapes=[pltpu.VMEM((2, tq, LANES), F32)] * 3,
        compiler_params=pltpu.CompilerParams(dimension_semantics=("arbitrary",) * 3, vmem_limit_bytes=VMEM_LIMIT),
        name="prompt_attn_mla" if mla else "prompt_attn_diff",
    )(q, k, v, kmeta, vmeta, gain, lam)


Q_ABS_W = 3 * LANES
PAGES_PER_STEP = 8


def _absorb_weights(w_ukv, g_kn_mla):
    wk = w_ukv.reshape(KV_LORA, MLA_HEADS, MLA_NOPE + MLA_V)[..., :MLA_NOPE]
    h = MLA_ROPE // 2
    gn, g1, g2 = g_kn_mla[:MLA_NOPE], g_kn_mla[MLA_NOPE:MLA_NOPE + h], g_kn_mla[MLA_NOPE + h:]
    m = jnp.zeros((MLA_HEADS, LANES, Q_ABS_W), F32)
    m = m.at[:, :MLA_NOPE, :KV_LORA].set(jnp.transpose(wk, (1, 2, 0)) * gn[None, :, None])
    i = jnp.arange(h)
    m = m.at[:, MLA_NOPE + i, LANES + i].set(g1)
    m = m.at[:, MLA_NOPE + h + i, LANES + h + i].set(g2)
    m = m.at[:, MLA_NOPE + h + i, 2 * LANES + i].set(g1)
    m = m.at[:, MLA_NOPE + i, 2 * LANES + h + i].set(-g2)
    return m.astype(BF16)


def _absorb_kernel(q_ref, m_ref, o_ref):
    n = q_ref.shape[0] // 8
    for h in range(MLA_HEADS):
        o_ref[:, h, :, :] = _dot(q_ref[:, h * LANES:(h + 1) * LANES], m_ref[h]).reshape(n, 8, Q_ABS_W)


def _absorb(qm, m_abs, n_new):
    t = qm.shape[0]
    tm = PAGE
    assert n_new == 8 and t % tm == 0
    return pl.pallas_call(
        _absorb_kernel,
        grid=(t // tm,),
        in_specs=[pl.BlockSpec((tm, MLA_HEADS * LANES), lambda i: (i, 0)),
                  pl.BlockSpec(m_abs.shape, lambda i: (0, 0, 0))],
        out_specs=pl.BlockSpec((tm // n_new, MLA_HEADS, n_new, Q_ABS_W), lambda i: (i, 0, 0, 0)),
        out_shape=jax.ShapeDtypeStruct((t // n_new, MLA_HEADS, n_new, Q_ABS_W), F32),
        compiler_params=pltpu.CompilerParams(dimension_semantics=("arbitrary",), vmem_limit_bytes=VMEM_LIMIT),
        name="absorb_q",
    )(qm, m_abs)


def _softmax_step(s, m_scr, l_scr, acc_scr):
    m_prev = m_scr[...]
    m_next = jnp.maximum(m_prev, jnp.max(s, axis=1, keepdims=True))
    alpha = jnp.exp2(m_prev - m_next)
    w = s.shape[1]
    p = jnp.exp2(s - _lane_tile(m_next, w))
    l_scr[...] = alpha * l_scr[...] + jnp.sum(p, axis=1, keepdims=True)
    acc_scr[...] = acc_scr[...] * alpha
    m_scr[...] = m_next
    return p


def _decode_kernel(pt_ref, qabs_ref, qrow_ref, knew_ref, vnew_ref, qd_ref, kdnew_ref, vdnew_ref, tabc_ref, tabs_ref,
                   wuk_ref, segt_ref, wuv_ref, go_ref, gsub_ref, lam_ref, *rest, n_pg, post_scale):
    ckv_refs, kr_refs, dk_refs, dv_refs = (rest[i * n_pg:(i + 1) * n_pg] for i in range(4))
    oa_ref, ob_ref, ma, la, acca, md, ld, accd, sa, sd = rest[4 * n_pg:]
    c = pl.program_id(1)
    n_rows = MLA_HEADS * 8

    @pl.when(c == 0)
    def _():
        for m_scr, l_scr, acc_scr in ((ma, la, acca), (md, ld, accd)):
            m_scr[...] = jnp.full(m_scr.shape, NEG_INF, F32)
            l_scr[...] = jnp.zeros(l_scr.shape, F32)
            acc_scr[...] = jnp.zeros(acc_scr.shape, F32)

    qabs = qabs_ref[...].reshape(n_rows, Q_ABS_W)
    q_lat = qabs[:, :KV_LORA]
    q_cos = qabs[:, LANES:LANES + MLA_ROPE]
    q_sin = qabs[:, 2 * LANES:2 * LANES + MLA_ROPE]
    qd = qd_ref[...]
    wuk = wuk_ref[...]
    segt = segt_ref[...]

    for p in range(n_pg):
        cols = slice(p * PAGE, (p + 1) * PAGE)
        ckv = ckv_refs[p][...]
        kr_t = kr_refs[p][...]
        knope = _dot(ckv, wuk)
        ss = _dot_nt(segt, knope * knope) + jnp.sum(kr_t * kr_t, axis=0, keepdims=True)
        r_t = lax.rsqrt(ss * (1.0 / MLA_QK) + RMS_EPS)
        f_cos = kr_t * tabc_ref[:, cols]
        f_sin = kr_t * tabs_ref[:, cols]
        s_pg = _dot_nt(q_lat, ckv) + _dot(q_cos, f_cos) + _dot(q_sin, f_sin)
        for h in range(MLA_HEADS):
            sa[h * 8:(h + 1) * 8, cols] = s_pg[h * 8:(h + 1) * 8, :] * r_t[h:h + 1, :]
        for m in range(DIFF_MAPS):
            sd[m * 8:(m + 1) * 8, cols] = _dot(qd[m * 8:(m + 1) * 8, :], dk_refs[p][m])

    pa = _softmax_step(sa[...], ma, la, acca)
    pd = _softmax_step(sd[...], md, ld, accd)
    acc_a = acca[...]
    for p in range(n_pg):
        cols = slice(p * PAGE, (p + 1) * PAGE)
        acc_a = acc_a + _dot(pa[:, cols], ckv_refs[p][...])
        for h in range(DIFF_HEADS):
            rows = slice(h * 16, (h + 1) * 16)
            v_h = dv_refs[p][pl.ds(h, PAGE, stride=DIFF_HEADS), :]
            accd[rows, :] = accd[rows, :] + _dot(pd[rows, cols], v_h)
    acca[...] = acc_a

    @pl.when(c == pl.num_programs(1) - 1)
    def _():
        ri = lax.broadcasted_iota(I32, (n_rows, n_rows), 0)
        ci = lax.broadcasted_iota(I32, (n_rows, n_rows), 1)
        valid = (ri // 8 == ci // 8) & (ci % 8 <= ri % 8)

        s_new = jnp.where(valid, _dot_nt(qrow_ref[...], knew_ref[...]), NEG_INF)
        p_new = _softmax_step(s_new, ma, la, acca)
        o_new = _dot(p_new, vnew_ref[...])
        lat = acca[...]
        lat_hi = lat.astype(BF16).astype(F32)
        lat_lo = lat - lat_hi
        l_a = la[...]
        for h in range(MLA_HEADS):
            rows = slice(h * 8, (h + 1) * 8)
            wv = wuv_ref[h]
            o = (_dot(lat_hi[rows], wv) + _dot(lat_lo[rows], wv) + o_new[rows]) / l_a[rows, :MLA_V]
            r = lax.rsqrt(jnp.mean(o * o, axis=1, keepdims=True) + RMS_EPS)
            oa_ref[rows, :] = (o * r * go_ref[...]).astype(oa_ref.dtype)

        sd_new = jnp.where(valid, _dot_nt(qd, kdnew_ref[...]), NEG_INF)
        pd_new = _softmax_step(sd_new, md, ld, accd)
        od = (accd[...] + _dot(pd_new, vdnew_ref[...])) / ld[...]
        lam = _lambda(lam_ref)
        for h in range(DIFF_HEADS):
            o = od[2 * h * 8:(2 * h + 1) * 8] - lam * od[(2 * h + 1) * 8:(2 * h + 2) * 8]
            r = lax.rsqrt(jnp.mean(o * o, axis=1, keepdims=True) + RMS_EPS)
            ob_ref[h * 8:(h + 1) * 8, :] = (o * r * gsub_ref[...] * post_scale).astype(ob_ref.dtype)


def _decode_attn(new, pool_ckv, pool_kr, pool_dk, pool_dv, page_table, w_ukv, gains, g_o, g_subln, lam, lam_init,
                 n_seq, n_new):
    assert n_new == 8
    n_pages = page_table.shape[1]
    n_pg = min(PAGES_PER_STEP, n_pages)
    assert n_pages % n_pg == 0
    past_len = n_pages * PAGE
    qm, km, vm, dq, dk16, dv16 = new[4:10]

    def rows_by_head(x, n_h):
        w = x.shape[1] // n_h
        x = jnp.transpose(x.reshape(n_seq, n_new, n_h, w), (0, 2, 1, 3))
        return x.reshape(n_seq, n_h * n_new, w).astype(F32)

    qabs = _absorb(qm, _absorb_weights(w_ukv, gains[1]), n_new)
    qrow = rows_by_head(qm, MLA_HEADS)
    knew = rows_by_head(km, MLA_HEADS)
    vnew = rows_by_head(vm, MLA_HEADS)
    qd = rows_by_head(dq, DIFF_MAPS)
    kdnew = rows_by_head(dk16, DIFF_MAPS)
    vdnew = jnp.repeat(rows_by_head(dv16, DIFF_HEADS).reshape(n_seq, DIFF_HEADS, n_new, DIFF_V), 2, axis=1)
    vdnew = vdnew.reshape(n_seq, DIFF_MAPS * n_new, DIFF_V)

    pool_kr = jnp.transpose(pool_kr, (0, 2, 1))
    pool_dk = jnp.transpose(pool_dk, (0, 2, 3, 1))
    pool_dv = pool_dv.reshape(pool_dv.shape[0], PAGE * DIFF_HEADS, DIFF_V)
    cs, sn = _rope_cs(jnp.arange(past_len), MLA_ROPE)
    tabc = jnp.concatenate([cs, cs], axis=1).T
    tabs = jnp.concatenate([sn, sn], axis=1).T
    wkv = w_ukv.reshape(KV_LORA, MLA_HEADS, MLA_NOPE + MLA_V)
    wuk = wkv[..., :MLA_NOPE].reshape(KV_LORA, MLA_HEADS * MLA_NOPE)
    wuv = jnp.transpose(wkv[..., MLA_NOPE:], (1, 0, 2))
    segt = (jnp.arange(MLA_HEADS)[:, None] == jnp.arange(MLA_HEADS * MLA_NOPE)[None, :] // MLA_NOPE).astype(F32)

    per_seq = lambda a: pl.BlockSpec((None,) + a.shape[1:], lambda s, c, pt: (s,) + (0,) * (a.ndim - 1))
    fixed = lambda a: pl.BlockSpec(a.shape, lambda s, c, pt: (0,) * a.ndim)
    tab_spec = pl.BlockSpec((MLA_ROPE, n_pg * PAGE), lambda s, c, pt: (0, c))

    def page_spec(pool, p):
        tail = pool.shape[1:]
        return pl.BlockSpec((None,) + tail, lambda s, c, pt: (pt[s, c * n_pg + p],) + (0,) * len(tail))

    seq_in = (qabs, qrow, knew, vnew, qd, kdnew, vdnew)
    consts = (wuk, segt, wuv, g_o[:, :MLA_V], g_subln, lam)
    pools = (pool_ckv, pool_kr, pool_dk, pool_dv)
    n_rows = MLA_HEADS * n_new
    grid_spec = pltpu.PrefetchScalarGridSpec(
        num_scalar_prefetch=1,
        grid=(n_seq, n_pages // n_pg),
        in_specs=[per_seq(a) for a in seq_in] + [tab_spec, tab_spec] + [fixed(a) for a in consts]
        + [page_spec(pool, p) for pool in pools for p in range(n_pg)],
        out_specs=[pl.BlockSpec((None, n_rows, MLA_V), lambda s, c, pt: (s, 0, 0)),
                   pl.BlockSpec((None, DIFF_HEADS * n_new, DIFF_V), lambda s, c, pt: (s, 0, 0))],
        scratch_shapes=[pltpu.VMEM((n_rows, LANES), F32)] * 6
        + [pltpu.VMEM((n_rows, n_pg * PAGE), F32)] * 2,
    )
    oa, ob = pl.pallas_call(
        functools.partial(_decode_kernel, n_pg=n_pg, post_scale=1.0 - lam_init),
        grid_spec=grid_spec,
        out_shape=[jax.ShapeDtypeStruct((n_seq, n_rows, MLA_V), BF16),
                   jax.ShapeDtypeStruct((n_seq, DIFF_HEADS * n_new, DIFF_V), BF16)],
        compiler_params=pltpu.CompilerParams(dimension_semantics=("arbitrary", "arbitrary"),
                                             vmem_limit_bytes=VMEM_LIMIT),
        name="decode_attn",
    )(page_table, *seq_in, tabc, tabs, *consts, *[pool for pool in pools for _ in range(n_pg)])

    def tokens_major(o, n_h):
        w = o.shape[2]
        return jnp.transpose(o.reshape(n_seq, n_h, n_new, w), (0, 2, 1, 3)).reshape(n_seq * n_new, n_h * w)

    return tokens_major(oa, MLA_HEADS), tokens_major(ob, DIFF_HEADS)


def _split_bf16(x):
    hi = x.astype(BF16)
    return hi, (x - hi.astype(F32)).astype(BF16)


def _out_gate_kernel(x_ref, oa_ref, ob_ref, woa_ref, wob_ref, gffn_ref, wg_hi_ref, wg_lo_ref, bias_ref,
                     hp_ref, gsel_ref, comb_ref):
    hp = x_ref[...] + _dot(oa_ref[...], woa_ref[...]) + _dot(ob_ref[...], wob_ref[...])
    hp_ref[...] = hp
    h = hp * lax.rsqrt(jnp.mean(hp * hp, axis=-1, keepdims=True) + RMS_EPS) * gffn_ref[...]
    h_hi, h_lo = _split_bf16(h)
    w_hi = wg_hi_ref[...]
    lt = _dot_nt(w_hi, h_hi) + _dot_nt(w_hi, h_lo) + _dot_nt(wg_lo_ref[...], h_hi) + bias_ref[...]

    gl = [lt[g:g + 1, :] for g in range(N_GROUPS)]
    gmax = functools.reduce(jnp.maximum, gl)
    gsel = jnp.full(gmax.shape, N_GROUPS - 1, I32)
    for g in range(N_GROUPS - 2, -1, -1):
        gsel = jnp.where(gl[g] == gmax, g, gsel)
    gw = 1.0 / functools.reduce(lambda a, b: a + b, [jnp.exp(x - gmax) for x in gl])

    base = 8
    el = []
    for e in range(EXP_PER_GROUP):
        v = lt[base + (N_GROUPS - 1) * EXP_PER_GROUP + e:base + (N_GROUPS - 1) * EXP_PER_GROUP + e + 1, :]
        for g in range(N_GROUPS - 2, -1, -1):
            v = jnp.where(gsel == g, lt[base + g * EXP_PER_GROUP + e:base + g * EXP_PER_GROUP + e + 1, :], v)
        el.append(v)
    v1 = functools.reduce(jnp.maximum, el)
    i1 = jnp.full(v1.shape, EXP_PER_GROUP - 1, I32)
    for e in range(EXP_PER_GROUP - 2, -1, -1):
        i1 = jnp.where(el[e] == v1, e, i1)
    rest = [jnp.where(i1 == e, -jnp.inf, el[e]) for e in range(EXP_PER_GROUP)]
    v2 = functools.reduce(jnp.maximum, rest)
    i2 = jnp.full(v1.shape, EXP_PER_GROUP - 1, I32)
    for e in range(EXP_PER_GROUP - 2, -1, -1):
        i2 = jnp.where(rest[e] == v2, e, i2)
    e2 = jnp.exp(v2 - v1)
    w1 = gw / (1.0 + e2)
    w2 = gw * e2 / (1.0 + e2)
    gsel_ref[...] = gsel
    for e in range(EXP_PER_GROUP):
        comb_ref[e:e + 1, :] = jnp.where(i1 == e, w1, jnp.where(i2 == e, w2, 0.0))


def _out_gate_both_kernel(xp_ref, oap_ref, obp_ref, xs_ref, oas_ref, obs_ref, *rest, n_prompt_tiles):
    i = pl.program_id(0)

    @pl.when(i < n_prompt_tiles)
    def _():
        _out_gate_kernel(xp_ref, oap_ref, obp_ref, *rest)

    @pl.when(i >= n_prompt_tiles)
    def _():
        _out_gate_kernel(xs_ref, oas_ref, obs_ref, *rest)


def _out_gate(prompt, sample, consts):
    (xp, oap, obp), (xs, oas, obs) = prompt, sample
    (n_p, d), n_s = xp.shape, xs.shape[0]
    tm = math.gcd(256, n_s)
    assert n_p % tm == 0
    npt, n_t = n_p // tm, (n_p + n_s) // tm
    p_row = lambda i: (jnp.minimum(i, npt - 1), 0)
    s_row = lambda i: (jnp.maximum(i - npt, 0), 0)
    fixed = lambda i: (0, 0)
    specs = lambda arrs, idx: [pl.BlockSpec((tm, a.shape[1]), idx) for a in arrs]
    return pl.pallas_call(
        functools.partial(_out_gate_both_kernel, n_prompt_tiles=npt),
        grid=(n_t,),
        in_specs=specs(prompt, p_row) + specs(sample, s_row) + [pl.BlockSpec(c.shape, fixed) for c in consts],
        out_specs=[pl.BlockSpec((tm, d), lambda i: (i, 0)), pl.BlockSpec((1, tm), lambda i: (0, i)),
                   pl.BlockSpec((EXP_PER_GROUP, tm), lambda i: (0, i))],
        out_shape=[jax.ShapeDtypeStruct((n_p + n_s, d), F32), jax.ShapeDtypeStruct((1, n_p + n_s), I32),
                   jax.ShapeDtypeStruct((EXP_PER_GROUP, n_p + n_s), F32)],
        compiler_params=pltpu.CompilerParams(dimension_semantics=("arbitrary",), vmem_limit_bytes=VMEM_LIMIT),
        name="out_gate",
    )(xp, oap, obp, xs, oas, obs, *consts)


def _moe_kernel(tg_ref, src_ref, dst_ref, hp_ref, comb_ref, gffn_ref, w1_ref, w3_ref, w2_ref,
                y_ref, xbuf, obuf, sem_in, sem_out, *, tmoe):
    t = pl.program_id(0)
    n_t = pl.num_programs(0)
    slot = t % 2
    other = 1 - slot
    nxt = jnp.minimum(t + 1, n_t - 1)
    chunk = tmoe // EXP_PER_GROUP

    def gather(tile, buf_slot, r):
        row = src_ref[tile * tmoe + r]
        pltpu.make_async_copy(hp_ref.at[pl.ds(row, 1)], xbuf.at[buf_slot, pl.ds(r, 1)], sem_in.at[buf_slot]).start()

    def scatter(tile_shifted, buf_slot, r):
        row = dst_ref[tile_shifted * tmoe + r]
        pltpu.make_async_copy(obuf.at[buf_slot, pl.ds(r, 1)], y_ref.at[pl.ds(row, 1)], sem_out.at[buf_slot]).start()

    def wait_gathers(buf_slot):
        pltpu.make_async_copy(hp_ref.at[pl.ds(0, tmoe)], xbuf.at[buf_slot], sem_in.at[buf_slot]).wait()

    def wait_scatters(buf_slot):
        pltpu.make_async_copy(obuf.at[buf_slot], y_ref.at[pl.ds(0, tmoe)], sem_out.at[buf_slot]).wait()

    @pl.when(t == 0)
    def _():
        obuf[...] = jnp.zeros_like(obuf)

        def first(r, c):
            gather(0, 0, r)
            return c

        lax.fori_loop(0, tmoe, first, 0, unroll=8)

    wait_gathers(slot)
    x = xbuf[slot]
    h = (x * lax.rsqrt(jnp.mean(x * x, axis=-1, keepdims=True) + RMS_EPS) * gffn_ref[...]).astype(BF16)
    comb = comb_ref[...]
    y = x
    for e in range(EXP_PER_GROUP):
        for r in range(e * chunk, (e + 1) * chunk):
            gather(nxt, other, r)
            scatter(t, other, r)
        a = _dot(h, w1_ref[e])
        b = _dot(h, w3_ref[e])
        hid = a * (1.0 / (1.0 + jnp.exp(-a))) * b * comb[:, e:e + 1]
        y = y + _dot(hid.astype(BF16), w2_ref[e])
    wait_scatters(other)
    obuf[slot] = y

    @pl.when(t == n_t - 1)
    def _():
        def last(r, c):
            scatter(t + 1, slot, r)
            return c

        lax.fori_loop(0, tmoe, last, 0, unroll=8)
        wait_scatters(slot)
        wait_gathers(other)


def _moe(hp_all, n_all, gsel, comb, g_ffn, w1, w3, w2, tmoe):
    d = hp_all.shape[1]
    n_tiles = n_all // tmoe + N_GROUPS
    onehot = (gsel[:, None] == jnp.arange(N_GROUPS, dtype=I32)[None, :]).astype(I32)
    csum = jnp.cumsum(onehot, axis=0)
    rank = jnp.take_along_axis(csum, gsel[:, None], axis=1)[:, 0] - 1
    g_tiles = (csum[-1] + tmoe - 1) // tmoe
    g_first = jnp.cumsum(g_tiles) - g_tiles
    dest = g_first[gsel] * tmoe + rank
    row_tok = jnp.full((n_tiles * tmoe,), -1, I32).at[dest].set(jnp.arange(n_all, dtype=I32))
    tile_group = jnp.sum(jnp.arange(n_tiles, dtype=I32)[:, None] >= g_first[None, 1:], axis=1).astype(I32)
    comb_sorted = jnp.where(row_tok[:, None] >= 0, comb.T[jnp.maximum(row_tok, 0)], 0.0)
    spare = n_all + jnp.arange(tmoe, dtype=I32)
    src_row = jnp.maximum(row_tok, 0)
    dst_row = jnp.where(row_tok >= 0, row_tok, jnp.tile(spare, n_tiles))
    dst_row = jnp.concatenate([spare, dst_row])

    w1g, w3g, w2g = w1.astype(BF16), w3.astype(BF16), w2.astype(BF16)
    group_w = lambda w: pl.BlockSpec((None,) + w.shape[1:], lambda t, tg, src, dst: (tg[t], 0, 0, 0))

    grid_spec = pltpu.PrefetchScalarGridSpec(
        num_scalar_prefetch=3,
        grid=(n_tiles,),
        in_specs=[pl.BlockSpec(memory_space=pl.ANY),
                  pl.BlockSpec((tmoe, EXP_PER_GROUP), lambda t, tg, src, dst: (t, 0)),
                  pl.BlockSpec((1, d), lambda t, tg, src, dst: (0, 0)),
                  group_w(w1g), group_w(w3g), group_w(w2g)],
        out_specs=pl.BlockSpec(memory_space=pl.ANY),
        scratch_shapes=[pltpu.VMEM((2, tmoe, d), F32), pltpu.VMEM((2, tmoe, d), F32),
                        pltpu.SemaphoreType.DMA((2,)), pltpu.SemaphoreType.DMA((2,))],
    )
    return pl.pallas_call(
        functools.partial(_moe_kernel, tmoe=tmoe),
        grid_spec=grid_spec,
        out_shape=jax.ShapeDtypeStruct((n_all + tmoe, d), F32),
        compiler_params=pltpu.CompilerParams(dimension_semantics=("arbitrary",), vmem_limit_bytes=VMEM_LIMIT),
        name="moe",
    )(tile_group, src_row, dst_row, hp_all, comb_sorted, g_ffn, w1g, w3g, w2g)


def kernel(x_prompt, x_sample, cache_ckv, cache_krope, cache_dk, cache_dv, page_table, meta_tokens, g_attn, w_in, g_cq, g_ckv, w_uq, g_qn_mla, w_ukv, g_kn_mla, g_o_mla, g_qn_diff, g_kn_diff, lambda_q1, lambda_k1, lambda_q2, lambda_k2, g_subln, w_o, g_ffn, w_gate_group, b_gate_group, w_gate_expert, b_gate_expert, w1, w3, w2):
    assert g_attn.shape[0] == 1, "single layer step"
    n_b, seq, d = x_prompt.shape
    n_seq, n_new, _ = x_sample.shape
    n_meta = meta_tokens.shape[0]
    past_len = page_table.shape[1] * PAGE
    lam_init = 0.8 - 0.6 * math.exp(-0.3 * 0)

    w_in_p, wq_p, wkv_p = _prep_proj_weights(w_in[0], w_uq[0], w_ukv[0])
    pconsts = (g_attn, w_in_p, g_cq, g_ckv, wq_p, wkv_p, _seg_ones(LANES), _seg_ones(DIFF_HD))
    gains = (g_qn_mla[0], g_kn_mla[0], g_qn_diff[0], g_kn_diff[0])
    tab_meta = _tables(jnp.arange(n_meta), *gains)
    tab_prompt = _tables(n_meta + jnp.arange(seq), *gains)
    tab_new = jnp.tile(_tables(past_len + jnp.arange(n_new), *gains), (PAGE // n_new, 1))

    tm = min(256, seq)
    meta = _project(meta_tokens.astype(F32), tab_meta, 1, n_meta, pconsts)
    main = _project(x_prompt.reshape(n_b * seq, d), tab_prompt, n_b, tm, pconsts)
    new = _project(x_sample.reshape(n_seq * n_new, d), tab_new, (n_seq * n_new) // PAGE, PAGE, pconsts)

    lam = jnp.concatenate([lambda_q1, lambda_k1, lambda_q2, lambda_k2, jnp.full((1, DIFF_HD), lam_init, F32)], axis=0)
    g_o = jnp.tile(g_o_mla, (1, LANES // MLA_V))
    tq = min(512, seq)
    oa_p = _prompt_attn(main[4], main[5], main[6], meta[5], meta[6], g_o, lam, n_b, tq, True, 1.0)
    ob_p = _prompt_attn(main[7], main[8], main[9], meta[8], meta[9], g_subln, lam, n_b, tq, False, 1.0 - lam_init)

    oa_s, ob_s = _decode_attn(new, cache_ckv[0], cache_krope[0], cache_dk[0], cache_dv[0], page_table,
                              w_ukv[0], gains, g_o, g_subln, lam, lam_init, n_seq, n_new)

    wo = w_o[0].astype(BF16)
    n_a = MLA_HEADS * MLA_V
    wg = jnp.concatenate([w_gate_group[0].T, jnp.zeros((8 - N_GROUPS, d), F32),
                          w_gate_expert[0].reshape(d, N_GROUPS * EXP_PER_GROUP).T], axis=0)
    wg_hi = wg.astype(BF16)
    wg_lo = (wg - wg_hi.astype(F32)).astype(BF16)
    bias = jnp.concatenate([b_gate_group[0], jnp.zeros((8 - N_GROUPS,), F32), b_gate_expert[0].reshape(-1)])[:, None]
    gconsts = (wo[:n_a], wo[n_a:], g_ffn, wg_hi, wg_lo, bias)
    n_p, n_s = n_b * seq, n_seq * n_new
    n_all = n_p + n_s
    hp_all, gsel, comb = _out_gate((x_prompt.reshape(n_p, d), oa_p, ob_p), (x_sample.reshape(n_s, d), oa_s, ob_s),
                                   gconsts)
    y_all = _moe(hp_all, n_all, gsel[0], comb, g_ffn, w1[0], w3[0], w2[0], 256)
    y_p, y_s = y_all[:n_p], y_all[n_p:n_all]

    def with_meta(m, x, tail):
        m = jnp.broadcast_to(m[None], (n_b,) + m.shape)
        return jnp.concatenate([m, x.reshape((n_b, seq) + m.shape[2:])], axis=1).reshape((1, n_b, n_meta + seq) + tail)

    return (y_p.reshape(n_b, seq, d), y_s.reshape(n_seq, n_new, d),
            with_meta(meta[0], main[0], (KV_LORA,)), with_meta(meta[1], main[1], (MLA_ROPE,)),
            with_meta(meta[2], main[2], (DIFF_MAPS, DIFF_HD)), with_meta(meta[3], main[3], (DIFF_HEADS, DIFF_V)),
            new[0].reshape(1, n_seq, n_new, KV_LORA), new[1].reshape(1, n_seq, n_new, MLA_ROPE),
            new[2].reshape(1, n_seq, n_new, DIFF_MAPS, DIFF_HD), new[3].reshape(1, n_seq, n_new, DIFF_HEADS, DIFF_V))
```

```python
import functools
import math

import jax
import jax.numpy as jnp
from jax import lax
from jax.experimental import pallas as pl
from jax.experimental.pallas import tpu as pltpu

F32 = jnp.float32
BF16 = jnp.bfloat16
I32 = jnp.int32

RMS_EPS = 1e-6
ROPE_THETA = 10000.0
NEG_INF = -1e30
PAGE = 128
LANES = 128

MLA_HEADS = 8
MLA_NOPE = 64
MLA_ROPE = 32
MLA_QK = MLA_NOPE + MLA_ROPE
MLA_V = 64
Q_LORA = 256
KV_LORA = 128
DIFF_HEADS = 4
DIFF_HD = 64
DIFF_V = 128
DIFF_MAPS = 2 * DIFF_HEADS
N_GROUPS = 4
EXP_PER_GROUP = 8
D_EXPERT = 256

VMEM_LIMIT = 48 * 1024 * 1024

N_TAB = 12


def _dot(a, b):
    return jnp.dot(a, b, preferred_element_type=F32)


def _dot_nt(a, b):
    return lax.dot_general(a, b, (((1,), (1,)), ((), ())), preferred_element_type=F32)


def _rope_cs(pos, d):
    inv = 1.0 / (ROPE_THETA ** (jnp.arange(0, d, 2, dtype=F32) / d))
    ang = pos.astype(F32)[:, None] * inv[None, :]
    return jnp.cos(ang), jnp.sin(ang)


def _mla_tables(pos, g, scale):
    c, s = _rope_cs(pos, MLA_ROPE)
    n = pos.shape[0]
    h = MLA_ROPE // 2
    gn, g1, g2 = g[:MLA_NOPE], g[MLA_NOPE:MLA_NOPE + h], g[MLA_NOPE + h:]
    z = lambda w: jnp.zeros((n, w), F32)
    big_c = jnp.concatenate([jnp.broadcast_to(gn[None], (n, MLA_NOPE)), c * g1, c * g2, z(LANES - MLA_QK)], 1)
    big_a = jnp.concatenate([z(MLA_NOPE), -s * g2, z(LANES - MLA_NOPE - h)], 1)
    big_b = jnp.concatenate([z(MLA_NOPE + h), s * g1, z(LANES - MLA_QK)], 1)
    return [big_c * scale, big_a * scale, big_b * scale]


def _diff_tables(pos, g, scale):
    c, s = _rope_cs(pos, DIFF_HD)
    n = pos.shape[0]
    h = DIFF_HD // 2
    z = jnp.zeros((n, h), F32)
    c64 = jnp.concatenate([c * g[:h], c * g[h:]], 1)
    a64 = jnp.concatenate([-s * g[h:], z], 1)
    b64 = jnp.concatenate([z, s * g[:h]], 1)
    return [jnp.tile(t, (1, 2)) * scale for t in (c64, a64, b64)]


def _tables(pos, g_qn_mla, g_kn_mla, g_qn_diff, g_kn_diff):
    log2e = math.log2(math.e)
    t = (_mla_tables(pos, g_qn_mla, MLA_QK ** -0.5 * log2e) + _mla_tables(pos, g_kn_mla, 1.0)
         + _diff_tables(pos, g_qn_diff, DIFF_HD ** -0.5 * log2e) + _diff_tables(pos, g_kn_diff, 1.0))
    return jnp.concatenate(t, axis=1)


def _prep_proj_weights(w_in, w_uq, w_ukv):
    d = w_in.shape[0]
    off_ckv, off_kr = Q_LORA, Q_LORA + KV_LORA
    off_rest = off_kr + MLA_ROPE
    kr = w_in[:, off_kr:off_rest]
    z = jnp.zeros((d, MLA_ROPE), w_in.dtype)
    w_in_p = jnp.concatenate([w_in[:, :off_kr], kr, z, kr, z, w_in[:, off_rest:]], axis=1).astype(BF16)
    wq = w_uq.reshape(Q_LORA, MLA_HEADS, MLA_QK)
    wq = jnp.pad(wq, ((0, 0), (0, 0), (0, LANES - MLA_QK))).reshape(Q_LORA, MLA_HEADS * LANES).astype(BF16)
    wkv = w_ukv.reshape(KV_LORA, MLA_HEADS, MLA_NOPE + MLA_V)
    wk = jnp.pad(wkv[..., :MLA_NOPE], ((0, 0), (0, 0), (0, LANES - MLA_NOPE))).reshape(KV_LORA, MLA_HEADS * LANES)
    wv = wkv[..., MLA_NOPE:].reshape(KV_LORA, MLA_HEADS * MLA_V)
    wkv_p = jnp.concatenate([wk, wv], axis=1).astype(BF16)
    return w_in_p, wq, wkv_p


def _seg_ones(width):
    i = jnp.arange(LANES)
    return (i[:, None] // width == i[None, :] // width).astype(BF16)


def _segsum(sq, e):
    hi = sq.astype(BF16)
    lo = (sq - hi.astype(F32)).astype(BF16)
    return _dot(hi, e) + _dot(lo, e)


def _norm_rope(x, tab_ref, t0, e, inv_n, sh_up, sh_dn):
    r = lax.rsqrt(_segsum(x * x, e) * inv_n + RMS_EPS)
    c = tab_ref[:, (t0 + 0) * LANES:(t0 + 1) * LANES]
    a = tab_ref[:, (t0 + 1) * LANES:(t0 + 2) * LANES]
    b = tab_ref[:, (t0 + 2) * LANES:(t0 + 3) * LANES]
    rot = x * c + pltpu.roll(x, sh_up, 1) * a + pltpu.roll(x, sh_dn, 1) * b
    return rot * r


def _proj_kernel(x_ref, tab_ref, gattn_ref, win_ref, gcq_ref, gckv_ref, wuq_ref, wkv_ref, e128_ref, e64_ref,
                 ckv_ref, kr_ref, dk32_ref, dv32_ref, qm_ref, km_ref, vm_ref, dq_ref, dk16_ref, dv16_ref):
    x = x_ref[...]
    u = x * lax.rsqrt(jnp.mean(x * x, axis=-1, keepdims=True) + RMS_EPS) * gattn_ref[...]
    proj = _dot(u.astype(BF16), win_ref[...])
    o_ckv, o_kr, o_dq = Q_LORA, Q_LORA + KV_LORA, Q_LORA + KV_LORA + LANES
    o_dk = o_dq + DIFF_MAPS * DIFF_HD
    o_dv = o_dk + DIFF_MAPS * DIFF_HD

    cq = proj[:, :Q_LORA]
    cqn = cq * lax.rsqrt(jnp.mean(cq * cq, axis=-1, keepdims=True) + RMS_EPS) * gcq_ref[...]
    ckv = proj[:, o_ckv:o_kr]
    ckvn = ckv * lax.rsqrt(jnp.mean(ckv * ckv, axis=-1, keepdims=True) + RMS_EPS) * gckv_ref[...]
    ckv_ref[...] = ckvn
    krs = proj[:, o_kr:o_dq]
    kr_ref[...] = krs[:, :MLA_ROPE]

    e128 = e128_ref[...]
    e64 = e64_ref[...]
    up, dn = LANES - MLA_ROPE // 2, MLA_ROPE // 2
    qm = _dot(cqn.astype(BF16), wuq_ref[...])
    for h in range(MLA_HEADS):
        sl = slice(h * LANES, (h + 1) * LANES)
        qm_ref[:, sl] = _norm_rope(qm[:, sl], tab_ref, 0, e128, 1.0 / MLA_QK, up, dn).astype(BF16)

    kv = _dot(ckvn.astype(BF16), wkv_ref[...])
    lane = lax.broadcasted_iota(I32, (1, LANES), 1)
    kr_mid = krs * ((lane >= MLA_NOPE) & (lane < MLA_QK)).astype(F32)
    for h in range(MLA_HEADS):
        sl = slice(h * LANES, (h + 1) * LANES)
        km_ref[:, sl] = _norm_rope(kv[:, sl] + kr_mid, tab_ref, 3, e128, 1.0 / MLA_QK, up, dn).astype(BF16)
    vm_ref[...] = kv[:, MLA_HEADS * LANES:].astype(BF16)

    up, dn = LANES - DIFF_HD // 2, DIFF_HD // 2
    for j in range(DIFF_MAPS * DIFF_HD // LANES):
        sl = slice(j * LANES, (j + 1) * LANES)
        dq_ref[:, sl] = _norm_rope(proj[:, o_dq + j * LANES:o_dq + (j + 1) * LANES], tab_ref, 6, e64,
                                   1.0 / DIFF_HD, up, dn).astype(BF16)
        dk = _norm_rope(proj[:, o_dk + j * LANES:o_dk + (j + 1) * LANES], tab_ref, 9, e64, 1.0 / DIFF_HD, up, dn)
        dk32_ref[:, sl] = dk
        dk16_ref[:, sl] = dk.astype(BF16)
    dv = proj[:, o_dv:]
    dv32_ref[...] = dv
    dv16_ref[...] = dv.astype(BF16)


def _project(x, tab, n_batch, tm, consts):
    t, d = x.shape
    n_i = t // n_batch // tm
    row = lambda b, i: (b * n_i + i, 0)
    fixed = lambda b, i: (0, 0)
    gattn, win, gcq, gckv, wuq, wkv, e128, e64 = consts
    widths = [(KV_LORA, F32), (MLA_ROPE, F32), (DIFF_MAPS * DIFF_HD, F32), (DIFF_HEADS * DIFF_V, F32),
              (MLA_HEADS * LANES, BF16), (MLA_HEADS * LANES, BF16), (MLA_HEADS * MLA_V, BF16),
              (DIFF_MAPS * DIFF_HD, BF16), (DIFF_MAPS * DIFF_HD, BF16), (DIFF_HEADS * DIFF_V, BF16)]
    return pl.pallas_call(
        _proj_kernel,
        grid=(n_batch, n_i),
        in_specs=[pl.BlockSpec((tm, d), row), pl.BlockSpec((tm, N_TAB * LANES), lambda b, i: (i, 0))]
        + [pl.BlockSpec(c.shape, fixed) for c in consts],
        out_specs=[pl.BlockSpec((tm, w), row) for w, _ in widths],
        out_shape=[jax.ShapeDtypeStruct((t, w), dt) for w, dt in widths],
        compiler_params=pltpu.CompilerParams(dimension_semantics=("arbitrary", "arbitrary"),
                                             vmem_limit_bytes=VMEM_LIMIT),
        name="project",
    )(x, tab, *consts)


def _lane_tile(x, width):
    if width <= LANES:
        return x[:, :width]
    return jnp.concatenate([x] * (width // LANES), axis=1)


def _flash_update(a, q, k, v, m_scr, l_scr, acc_scr):
    s = _dot_nt(q, k)
    m_prev = m_scr[a]
    m_next = jnp.maximum(m_prev, jnp.max(s, axis=1, keepdims=True))
    alpha = jnp.exp2(m_prev - m_next)
    p = jnp.exp2(s - _lane_tile(m_next, s.shape[1]))
    l_scr[a] = alpha * l_scr[a] + jnp.sum(p, axis=1, keepdims=True)
    acc_scr[a] = acc_scr[a] * alpha + _dot(p.astype(BF16), v)
    m_scr[a] = m_next


def _pattn_kernel(q_ref, k_ref, v_ref, kmeta_ref, vmeta_ref, gain_ref, lam_ref, o_ref, m_scr, l_scr, acc_scr,
                  *, mla, tq, n_meta, post_scale):
    i = pl.program_id(2)
    lane = lax.broadcasted_iota(I32, (1, LANES), 1)
    lo_half = lane < LANES // 2
    if mla:
        qs = [q_ref[:, :LANES], q_ref[:, LANES:]]
        ksl = [slice(0, LANES), slice(LANES, 2 * LANES)]
    else:
        q = q_ref[...]
        qs = [q * lo_half.astype(BF16), q * (1 - lo_half.astype(I32)).astype(BF16)]
        ksl = [slice(0, LANES), slice(0, LANES)]

    k0 = pl.multiple_of(i * tq, tq)
    causal = lax.broadcasted_iota(I32, (tq, tq), 0) >= lax.broadcasted_iota(I32, (tq, tq), 1)
    meta_ok = lane < n_meta
    v_diag = v_ref[pl.ds(k0, tq), :]
    v_meta = vmeta_ref[...]
    for a in range(2):
        s1 = jnp.where(causal, _dot_nt(qs[a], k_ref[pl.ds(k0, tq), ksl[a]]), NEG_INF)
        s2 = jnp.where(meta_ok, _dot_nt(qs[a], kmeta_ref[:, ksl[a]]), NEG_INF)
        m = jnp.maximum(jnp.max(s1, axis=1, keepdims=True), jnp.max(s2, axis=1, keepdims=True))
        p1 = jnp.exp2(s1 - m)
        p2 = jnp.exp2(s2 - m)
        m_scr[a] = jnp.broadcast_to(m, (tq, LANES))
        l_scr[a] = jnp.broadcast_to(jnp.sum(p1, axis=1, keepdims=True) + jnp.sum(p2, axis=1, keepdims=True),
                                    (tq, LANES))
        acc_scr[a] = _dot(p1.astype(BF16), v_diag) + _dot(p2.astype(BF16), v_meta)

    def body(j, carry):
        kj = pl.multiple_of(j * tq, tq)
        v = v_ref[pl.ds(kj, tq), :]
        for a in range(2):
            _flash_update(a, qs[a], k_ref[pl.ds(kj, tq), ksl[a]], v, m_scr, l_scr, acc_scr)
        return carry

    lax.fori_loop(0, i, body, 0)

    o0 = acc_scr[0] / l_scr[0]
    o1 = acc_scr[1] / l_scr[1]
    if mla:
        o = jnp.where(lo_half, o0, o1)
        sq = o * o
        ss_lo = jnp.sum(jnp.where(lo_half, sq, 0.0), axis=1, keepdims=True)
        ss_hi = jnp.sum(jnp.where(lo_half, 0.0, sq), axis=1, keepdims=True)
        r = jnp.where(lo_half, lax.rsqrt(ss_lo * (2.0 / LANES) + RMS_EPS), lax.rsqrt(ss_hi * (2.0 / LANES) + RMS_EPS))
    else:
        o = o0 - _lambda(lam_ref) * o1
        r = lax.rsqrt(jnp.mean(o * o, axis=1, keepdims=True) + RMS_EPS)
    o_ref[...] = (o * r * gain_ref[...] * post_scale).astype(o_ref.dtype)


def _lambda(lam_ref):
    lam = lam_ref[...]
    a = jnp.exp(jnp.sum(lam[0:1] * lam[1:2], axis=1, keepdims=True))
    b = jnp.exp(jnp.sum(lam[2:3] * lam[3:4], axis=1, keepdims=True))
    return a - b + lam[4:5, 0:1]


def _prompt_attn(q, k, v, kmeta, vmeta, gain, lam, n_batch, tq, mla, post_scale):
    t = q.shape[0]
    s = t // n_batch
    n_q = s // tq
    n_meta = kmeta.shape[0]
    assert n_meta <= LANES
    kmeta = jnp.pad(kmeta, ((0, LANES - n_meta), (0, 0)))
    vmeta = jnp.pad(vmeta, ((0, LANES - n_meta), (0, 0)))
    kw = 2 * LANES if mla else LANES
    n_hp = q.shape[1] // kw
    kern = functools.partial(_pattn_kernel, mla=mla, tq=tq, n_meta=n_meta, post_scale=post_scale)
    return pl.pallas_call(
        kern,
        grid=(n_batch, n_hp, n_q),
        in_specs=[pl.BlockSpec((tq, kw), lambda b, h, i: (b * n_q + i, h)),
                  pl.BlockSpec((s, kw), lambda b, h, i: (b, h)),
                  pl.BlockSpec((s, LANES), lambda b, h, i: (b, h)),
                  pl.BlockSpec((LANES, kw), lambda b, h, i: (0, h)),
                  pl.BlockSpec((LANES, LANES), lambda b, h, i: (0, h)),
                  pl.BlockSpec((1, LANES), lambda b, h, i: (0, 0)),
                  pl.BlockSpec(lam.shape, lambda b, h, i: (0, 0))],
        out_specs=pl.BlockSpec((tq, LANES), lambda b, h, i: (b * n_q + i, h)),
        out_shape=jax.ShapeDtypeStruct((t, n_hp * LANES), BF16),
        scratch_shapes=[pltpu.VMEM((2, tq, LANES), F32)] * 3,
        compiler_params=pltpu.CompilerParams(dimension_semantics=("arbitrary",) * 3, vmem_limit_bytes=VMEM_LIMIT),
        name="prompt_attn_mla" if mla else "prompt_attn_diff",
    )(q, k, v, kmeta, vmeta, gain, lam)


Q_ABS_W = 2 * LANES
PAGES_PER_STEP = 16


def _absorb_weights(w_ukv, g_kn_mla):
    wk = w_ukv.reshape(KV_LORA, MLA_HEADS, MLA_NOPE + MLA_V)[..., :MLA_NOPE]
    h = MLA_ROPE // 2
    gn, g1, g2 = g_kn_mla[:MLA_NOPE], g_kn_mla[MLA_NOPE:MLA_NOPE + h], g_kn_mla[MLA_NOPE + h:]
    m = jnp.zeros((MLA_HEADS, LANES, Q_ABS_W), F32)
    m = m.at[:, :MLA_NOPE, :KV_LORA].set(jnp.transpose(wk, (1, 2, 0)) * gn[None, :, None])
    i = jnp.arange(h)
    m = m.at[:, MLA_NOPE + i, LANES + i].set(g1)
    m = m.at[:, MLA_NOPE + h + i, LANES + h + i].set(g2)
    m = m.at[:, MLA_NOPE + h + i, LANES + MLA_ROPE + i].set(g1)
    m = m.at[:, MLA_NOPE + i, LANES + MLA_ROPE + h + i].set(-g2)
    return m.astype(BF16)


def _absorb_kernel(q_ref, m_ref, o_ref):
    n = q_ref.shape[0] // 8
    for h in range(MLA_HEADS):
        o_ref[:, h, :, :] = _dot(q_ref[:, h * LANES:(h + 1) * LANES], m_ref[h]).reshape(n, 8, Q_ABS_W)


def _absorb(qm, m_abs, n_new):
    t = qm.shape[0]
    tm = PAGE
    assert n_new == 8 and t % tm == 0
    return pl.pallas_call(
        _absorb_kernel,
        grid=(t // tm,),
        in_specs=[pl.BlockSpec((tm, MLA_HEADS * LANES), lambda i: (i, 0)),
                  pl.BlockSpec(m_abs.shape, lambda i: (0, 0, 0))],
        out_specs=pl.BlockSpec((tm // n_new, MLA_HEADS, n_new, Q_ABS_W), lambda i: (i, 0, 0, 0)),
        out_shape=jax.ShapeDtypeStruct((t // n_new, MLA_HEADS, n_new, Q_ABS_W), F32),
        compiler_params=pltpu.CompilerParams(dimension_semantics=("arbitrary",), vmem_limit_bytes=VMEM_LIMIT),
        name="absorb_q",
    )(qm, m_abs)


def _softmax_step(s, m_scr, l_scr, acc_scr):
    m_prev = m_scr[...]
    m_next = jnp.maximum(m_prev, jnp.max(s, axis=1, keepdims=True))
    alpha = jnp.exp2(m_prev - m_next)
    w = s.shape[1]
    p = jnp.exp2(s - _lane_tile(m_next, w))
    l_scr[...] = alpha * l_scr[...] + jnp.sum(p, axis=1, keepdims=True)
    acc_scr[...] = acc_scr[...] * alpha
    m_scr[...] = m_next
    return p


def _decode_kernel(pt_ref, qabs_ref, qrow_ref, knew_ref, vnew_ref, qd_ref, kdnew_ref, vdnew_ref, tabc_ref, tabs_ref,
                   wukt_ref, wuv_ref, go_ref, gsub_ref, lam_ref, *rest, n_pg, post_scale):
    ckv_refs, kr_refs, dk_refs, dv_refs = (rest[i * n_pg:(i + 1) * n_pg] for i in range(4))
    oa_ref, ob_ref, ma, la, acca, md, ld, accd, sa, sd = rest[4 * n_pg:]
    c = pl.program_id(1)
    n_rows = MLA_HEADS * 8

    @pl.when(c == 0)
    def _():
        for m_scr, l_scr, acc_scr in ((ma, la, acca), (md, ld, accd)):
            m_scr[...] = jnp.full(m_scr.shape, NEG_INF, F32)
            l_scr[...] = jnp.zeros(l_scr.shape, F32)
            acc_scr[...] = jnp.zeros(acc_scr.shape, F32)

    qabs = qabs_ref[...].reshape(n_rows, Q_ABS_W)
    lhs_lat = jnp.concatenate([qabs[:, :KV_LORA], wukt_ref[...]], axis=0)
    q_cs = qabs[:, LANES:LANES + 2 * MLA_ROPE]
    qd = qd_ref[...]
    ri = lax.broadcasted_iota(I32, (n_rows, DIFF_MAPS * DIFF_HD), 0)
    ci = lax.broadcasted_iota(I32, (n_rows, DIFF_MAPS * DIFF_HD), 1)
    q_bd = jnp.where(ri // 8 == ci // DIFF_HD, jnp.concatenate([qd] * DIFF_MAPS, axis=1), 0.0)

    for p in range(n_pg):
        cols = slice(p * PAGE, (p + 1) * PAGE)
        kr_t = kr_refs[p][...]
        big = _dot_nt(lhs_lat, ckv_refs[p][...])
        kn_t = big[n_rows:]
        ss = jnp.sum((kn_t * kn_t).reshape(MLA_HEADS, MLA_NOPE, PAGE), axis=1)
        ss = ss + jnp.sum(kr_t * kr_t, axis=0, keepdims=True)
        r_t = lax.rsqrt(ss * (1.0 / MLA_QK) + RMS_EPS)
        f_cs = jnp.concatenate([kr_t * tabc_ref[:, cols], kr_t * tabs_ref[:, cols]], axis=0)
        s_pg = big[:n_rows] + _dot(q_cs, f_cs)
        for h in range(MLA_HEADS):
            sa[h * 8:(h + 1) * 8, cols] = s_pg[h * 8:(h + 1) * 8, :] * r_t[h:h + 1, :]
        sd[:, cols] = _dot(q_bd, dk_refs[p][...].reshape(DIFF_MAPS * DIFF_HD, PAGE))

    pa = _softmax_step(sa[...], ma, la, acca)
    pd = _softmax_step(sd[...], md, ld, accd)
    acc_a = acca[...]
    acc_d = accd[...]
    head_of_row = lax.broadcasted_iota(I32, (n_rows, PAGE), 0) // 16
    for p in range(n_pg):
        cols = slice(p * PAGE, (p + 1) * PAGE)
        acc_a = acc_a + _dot(pa[:, cols], ckv_refs[p][...])
        v_stack = jnp.concatenate([dv_refs[p][pl.ds(h, PAGE, stride=DIFF_HEADS), :] for h in range(DIFF_HEADS)],
                                  axis=0)
        pd_pg = pd[:, cols]
        p_wide = jnp.concatenate([jnp.where(head_of_row == h, pd_pg, 0.0) for h in range(DIFF_HEADS)], axis=1)
        acc_d = acc_d + _dot(p_wide, v_stack)
    acca[...] = acc_a
    accd[...] = acc_d

    @pl.when(c == pl.num_programs(1) - 1)
    def _():
        ri = lax.broadcasted_iota(I32, (n_rows, n_rows), 0)
        ci = lax.broadcasted_iota(I32, (n_rows, n_rows), 1)
        valid = (ri // 8 == ci // 8) & (ci % 8 <= ri % 8)

        s_new = jnp.where(valid, _dot_nt(qrow_ref[...], knew_ref[...]), NEG_INF)
        p_new = _softmax_step(s_new, ma, la, acca)
        o_new = _dot(p_new, vnew_ref[...])
        lat = acca[...]
        lat_hi = lat.astype(BF16).astype(F32)
        lat_lo = lat - lat_hi
        l_a = la[...]
        for h in range(MLA_HEADS):
            rows = slice(h * 8, (h + 1) * 8)
            wv = wuv_ref[h]
            o = (_dot(lat_hi[rows], wv) + _dot(lat_lo[rows], wv) + o_new[rows]) / l_a[rows, :MLA_V]
            r = lax.rsqrt(jnp.mean(o * o, axis=1, keepdims=True) + RMS_EPS)
            oa_ref[rows, :] = (o * r * go_ref[...]).astype(oa_ref.dtype)

        sd_new = jnp.where(valid, _dot_nt(qd, kdnew_ref[...]), NEG_INF)
        pd_new = _softmax_step(sd_new, md, ld, accd)
        od = (accd[...] + _dot(pd_new, vdnew_ref[...])) / ld[...]
        lam = _lambda(lam_ref)
        for h in range(DIFF_HEADS):
            o = od[2 * h * 8:(2 * h + 1) * 8] - lam * od[(2 * h + 1) * 8:(2 * h + 2) * 8]
            r = lax.rsqrt(jnp.mean(o * o, axis=1, keepdims=True) + RMS_EPS)
            ob_ref[h * 8:(h + 1) * 8, :] = (o * r * gsub_ref[...] * post_scale).astype(ob_ref.dtype)


def _decode_attn(new, pool_ckv, pool_kr, pool_dk, pool_dv, page_table, w_ukv, gains, g_o, g_subln, lam, lam_init,
                 n_seq, n_new):
    assert n_new == 8
    n_pages = page_table.shape[1]
    n_pg = min(PAGES_PER_STEP, n_pages)
    assert n_pages % n_pg == 0
    past_len = n_pages * PAGE
    qm, km, vm, dq, dk16, dv16 = new[4:10]

    def rows_by_head(x, n_h):
        w = x.shape[1] // n_h
        x = jnp.transpose(x.reshape(n_seq, n_new, n_h, w), (0, 2, 1, 3))
        return x.reshape(n_seq, n_h * n_new, w).astype(F32)

    qabs = _absorb(qm, _absorb_weights(w_ukv, gains[1]), n_new)
    qrow = rows_by_head(qm, MLA_HEADS)
    knew = rows_by_head(km, MLA_HEADS)
    vnew = rows_by_head(vm, MLA_HEADS)
    qd = rows_by_head(dq, DIFF_MAPS)
    kdnew = rows_by_head(dk16, DIFF_MAPS)
    vdnew = jnp.repeat(rows_by_head(dv16, DIFF_HEADS).reshape(n_seq, DIFF_HEADS, n_new, DIFF_V), 2, axis=1)
    vdnew = vdnew.reshape(n_seq, DIFF_MAPS * n_new, DIFF_V)

    pool_kr = jnp.transpose(pool_kr, (0, 2, 1))
    pool_dk = jnp.transpose(pool_dk, (0, 2, 3, 1))
    pool_dv = pool_dv.reshape(pool_dv.shape[0], PAGE * DIFF_HEADS, DIFF_V)
    cs, sn = _rope_cs(jnp.arange(past_len), MLA_ROPE)
    tabc = jnp.concatenate([cs, cs], axis=1).T
    tabs = jnp.concatenate([sn, sn], axis=1).T
    wkv = w_ukv.reshape(KV_LORA, MLA_HEADS, MLA_NOPE + MLA_V)
    wukt = wkv[..., :MLA_NOPE].reshape(KV_LORA, MLA_HEADS * MLA_NOPE).T
    wuv = jnp.transpose(wkv[..., MLA_NOPE:], (1, 0, 2))

    per_seq = lambda a: pl.BlockSpec((None,) + a.shape[1:], lambda s, c, pt: (s,) + (0,) * (a.ndim - 1))
    fixed = lambda a: pl.BlockSpec(a.shape, lambda s, c, pt: (0,) * a.ndim)
    tab_spec = pl.BlockSpec((MLA_ROPE, n_pg * PAGE), lambda s, c, pt: (0, c))

    def page_spec(pool, p):
        tail = pool.shape[1:]
        return pl.BlockSpec((None,) + tail, lambda s, c, pt: (pt[s, c * n_pg + p],) + (0,) * len(tail))

    seq_in = (qabs, qrow, knew, vnew, qd, kdnew, vdnew)
    consts = (wukt, wuv, g_o[:, :MLA_V], g_subln, lam)
    pools = (pool_ckv, pool_kr, pool_dk, pool_dv)
    n_rows = MLA_HEADS * n_new
    grid_spec = pltpu.PrefetchScalarGridSpec(
        num_scalar_prefetch=1,
        grid=(n_seq, n_pages // n_pg),
        in_specs=[per_seq(a) for a in seq_in] + [tab_spec, tab_spec] + [fixed(a) for a in consts]
        + [page_spec(pool, p) for pool in pools for p in range(n_pg)],
        out_specs=[pl.BlockSpec((None, n_rows, MLA_V), lambda s, c, pt: (s, 0, 0)),
                   pl.BlockSpec((None, DIFF_HEADS * n_new, DIFF_V), lambda s, c, pt: (s, 0, 0))],
        scratch_shapes=[pltpu.VMEM((n_rows, LANES), F32)] * 6
        + [pltpu.VMEM((n_rows, n_pg * PAGE), F32)] * 2,
    )
    oa, ob = pl.pallas_call(
        functools.partial(_decode_kernel, n_pg=n_pg, post_scale=1.0 - lam_init),
        grid_spec=grid_spec,
        out_shape=[jax.ShapeDtypeStruct((n_seq, n_rows, MLA_V), BF16),
                   jax.ShapeDtypeStruct((n_seq, DIFF_HEADS * n_new, DIFF_V), BF16)],
        compiler_params=pltpu.CompilerParams(dimension_semantics=("arbitrary", "arbitrary"),
                                             vmem_limit_bytes=VMEM_LIMIT),
        name="decode_attn",
    )(page_table, *seq_in, tabc, tabs, *consts, *[pool for pool in pools for _ in range(n_pg)])

    def tokens_major(o, n_h):
        w = o.shape[2]
        return jnp.transpose(o.reshape(n_seq, n_h, n_new, w), (0, 2, 1, 3)).reshape(n_seq * n_new, n_h * w)

    return tokens_major(oa, MLA_HEADS), tokens_major(ob, DIFF_HEADS)


def _split_bf16(x):
    hi = x.astype(BF16)
    return hi, (x - hi.astype(F32)).astype(BF16)


def _out_gate_kernel(x_ref, oa_ref, ob_ref, woa_ref, wob_ref, gffn_ref, wg_hi_ref, wg_lo_ref, bias_ref,
                     hp_ref, gsel_ref, comb_ref):
    hp = x_ref[...] + _dot(oa_ref[...], woa_ref[...]) + _dot(ob_ref[...], wob_ref[...])
    hp_ref[...] = hp
    h = hp * lax.rsqrt(jnp.mean(hp * hp, axis=-1, keepdims=True) + RMS_EPS) * gffn_ref[...]
    h_hi, h_lo = _split_bf16(h)
    w_hi = wg_hi_ref[...]
    lt = _dot_nt(w_hi, h_hi) + _dot_nt(w_hi, h_lo) + _dot_nt(wg_lo_ref[...], h_hi) + bias_ref[...]

    gl = [lt[g:g + 1, :] for g in range(N_GROUPS)]
    gmax = functools.reduce(jnp.maximum, gl)
    gsel = jnp.full(gmax.shape, N_GROUPS - 1, I32)
    for g in range(N_GROUPS - 2, -1, -1):
        gsel = jnp.where(gl[g] == gmax, g, gsel)
    gw = 1.0 / functools.reduce(lambda a, b: a + b, [jnp.exp(x - gmax) for x in gl])

    base = 8
    el = []
    for e in range(EXP_PER_GROUP):
        v = lt[base + (N_GROUPS - 1) * EXP_PER_GROUP + e:base + (N_GROUPS - 1) * EXP_PER_GROUP + e + 1, :]
        for g in range(N_GROUPS - 2, -1, -1):
            v = jnp.where(gsel == g, lt[base + g * EXP_PER_GROUP + e:base + g * EXP_PER_GROUP + e + 1, :], v)
        el.append(v)
    v1 = functools.reduce(jnp.maximum, el)
    i1 = jnp.full(v1.shape, EXP_PER_GROUP - 1, I32)
    for e in range(EXP_PER_GROUP - 2, -1, -1):
        i1 = jnp.where(el[e] == v1, e, i1)
    rest = [jnp.where(i1 == e, -jnp.inf, el[e]) for e in range(EXP_PER_GROUP)]
    v2 = functools.reduce(jnp.maximum, rest)
    i2 = jnp.full(v1.shape, EXP_PER_GROUP - 1, I32)
    for e in range(EXP_PER_GROUP - 2, -1, -1):
        i2 = jnp.where(rest[e] == v2, e, i2)
    e2 = jnp.exp(v2 - v1)
    w1 = gw / (1.0 + e2)
    w2 = gw * e2 / (1.0 + e2)
    gsel_ref[...] = gsel
    for e in range(EXP_PER_GROUP):
        comb_ref[e:e + 1, :] = jnp.where(i1 == e, w1, jnp.where(i2 == e, w2, 0.0))


def _out_gate_both_kernel(xp_ref, oap_ref, obp_ref, xs_ref, oas_ref, obs_ref, *rest, n_prompt_tiles):
    i = pl.program_id(0)

    @pl.when(i < n_prompt_tiles)
    def _():
        _out_gate_kernel(xp_ref, oap_ref, obp_ref, *rest)

    @pl.when(i >= n_prompt_tiles)
    def _():
        _out_gate_kernel(xs_ref, oas_ref, obs_ref, *rest)


def _out_gate(prompt, sample, consts):
    (xp, oap, obp), (xs, oas, obs) = prompt, sample
    (n_p, d), n_s = xp.shape, xs.shape[0]
    tm = math.gcd(256, n_s)
    assert n_p % tm == 0
    npt, n_t = n_p // tm, (n_p + n_s) // tm
    p_row = lambda i: (jnp.minimum(i, npt - 1), 0)
    s_row = lambda i: (jnp.maximum(i - npt, 0), 0)
    fixed = lambda i: (0, 0)
    specs = lambda arrs, idx: [pl.BlockSpec((tm, a.shape[1]), idx) for a in arrs]
    return pl.pallas_call(
        functools.partial(_out_gate_both_kernel, n_prompt_tiles=npt),
        grid=(n_t,),
        in_specs=specs(prompt, p_row) + specs(sample, s_row) + [pl.BlockSpec(c.shape, fixed) for c in consts],
        out_specs=[pl.BlockSpec((tm, d), lambda i: (i, 0)), pl.BlockSpec((1, tm), lambda i: (0, i)),
                   pl.BlockSpec((EXP_PER_GROUP, tm), lambda i: (0, i))],
        out_shape=[jax.ShapeDtypeStruct((n_p + n_s, d), F32), jax.ShapeDtypeStruct((1, n_p + n_s), I32),
                   jax.ShapeDtypeStruct((EXP_PER_GROUP, n_p + n_s), F32)],
        compiler_params=pltpu.CompilerParams(dimension_semantics=("arbitrary",), vmem_limit_bytes=VMEM_LIMIT),
        name="out_gate",
    )(xp, oap, obp, xs, oas, obs, *consts)


def _moe_kernel(tg_ref, src_ref, dst_ref, hp_ref, comb_ref, gffn_ref, w1_ref, w3_ref, w2_ref,
                y_ref, xbuf, obuf, sem_in, sem_out, *, tmoe):
    t = pl.program_id(0)
    n_t = pl.num_programs(0)
    slot = t % 2
    other = 1 - slot
    nxt = jnp.minimum(t + 1, n_t - 1)
    chunk = tmoe // EXP_PER_GROUP

    def gather(tile, buf_slot, r):
        row = src_ref[tile * tmoe + r]
        pltpu.make_async_copy(hp_ref.at[pl.ds(row, 1)], xbuf.at[buf_slot, pl.ds(r, 1)], sem_in.at[buf_slot]).start()

    def scatter(tile_shifted, buf_slot, r):
        row = dst_ref[tile_shifted * tmoe + r]
        pltpu.make_async_copy(obuf.at[buf_slot, pl.ds(r, 1)], y_ref.at[pl.ds(row, 1)], sem_out.at[buf_slot]).start()

    def wait_gathers(buf_slot):
        pltpu.make_async_copy(hp_ref.at[pl.ds(0, tmoe)], xbuf.at[buf_slot], sem_in.at[buf_slot]).wait()

    def wait_scatters(buf_slot):
        pltpu.make_async_copy(obuf.at[buf_slot], y_ref.at[pl.ds(0, tmoe)], sem_out.at[buf_slot]).wait()

    @pl.when(t == 0)
    def _():
        obuf[...] = jnp.zeros_like(obuf)

        def first(r, c):
            gather(0, 0, r)
            return c

        lax.fori_loop(0, tmoe, first, 0, unroll=8)

    wait_gathers(slot)
    x = xbuf[slot]
    h = (x * lax.rsqrt(jnp.mean(x * x, axis=-1, keepdims=True) + RMS_EPS) * gffn_ref[...]).astype(BF16)
    comb = comb_ref[...]
    y = x
    for e in range(EXP_PER_GROUP):
        for r in range(e * chunk, (e + 1) * chunk):
            gather(nxt, other, r)
            scatter(t, other, r)
        a = _dot(h, w1_ref[e])
        b = _dot(h, w3_ref[e])
        hid = a * (1.0 / (1.0 + jnp.exp(-a))) * b * comb[:, e:e + 1]
        y = y + _dot(hid.astype(BF16), w2_ref[e])
    wait_scatters(other)
    obuf[slot] = y

    @pl.when(t == n_t - 1)
    def _():
        def last(r, c):
            scatter(t + 1, slot, r)
            return c

        lax.fori_loop(0, tmoe, last, 0, unroll=8)
        wait_scatters(slot)
        wait_gathers(other)


def _moe(hp_all, n_all, gsel, comb, g_ffn, w1, w3, w2, tmoe):
    d = hp_all.shape[1]
    n_tiles = n_all // tmoe + N_GROUPS
    onehot = (gsel[:, None] == jnp.arange(N_GROUPS, dtype=I32)[None, :]).astype(I32)
    csum = jnp.cumsum(onehot, axis=0)
    rank = jnp.take_along_axis(csum, gsel[:, None], axis=1)[:, 0] - 1
    g_tiles = (csum[-1] + tmoe - 1) // tmoe
    g_first = jnp.cumsum(g_tiles) - g_tiles
    dest = g_first[gsel] * tmoe + rank
    row_tok = jnp.full((n_tiles * tmoe,), -1, I32).at[dest].set(jnp.arange(n_all, dtype=I32))
    tile_group = jnp.sum(jnp.arange(n_tiles, dtype=I32)[:, None] >= g_first[None, 1:], axis=1).astype(I32)
    comb_sorted = jnp.where(row_tok[:, None] >= 0, comb.T[jnp.maximum(row_tok, 0)], 0.0)
    spare = n_all + jnp.arange(tmoe, dtype=I32)
    src_row = jnp.maximum(row_tok, 0)
    dst_row = jnp.where(row_tok >= 0, row_tok, jnp.tile(spare, n_tiles))
    dst_row = jnp.concatenate([spare, dst_row])

    w1g, w3g, w2g = w1.astype(BF16), w3.astype(BF16), w2.astype(BF16)
    group_w = lambda w: pl.BlockSpec((None,) + w.shape[1:], lambda t, tg, src, dst: (tg[t], 0, 0, 0))

    grid_spec = pltpu.PrefetchScalarGridSpec(
        num_scalar_prefetch=3,
        grid=(n_tiles,),
        in_specs=[pl.BlockSpec(memory_space=pl.ANY),
                  pl.BlockSpec((tmoe, EXP_PER_GROUP), lambda t, tg, src, dst: (t, 0)),
                  pl.BlockSpec((1, d), lambda t, tg, src, dst: (0, 0)),
                  group_w(w1g), group_w(w3g), group_w(w2g)],
        out_specs=pl.BlockSpec(memory_space=pl.ANY),
        scratch_shapes=[pltpu.VMEM((2, tmoe, d), F32), pltpu.VMEM((2, tmoe, d), F32),
                        pltpu.SemaphoreType.DMA((2,)), pltpu.SemaphoreType.DMA((2,))],
    )
    return pl.pallas_call(
        functools.partial(_moe_kernel, tmoe=tmoe),
        grid_spec=grid_spec,
        out_shape=jax.ShapeDtypeStruct((n_all + tmoe, d), F32),
        compiler_params=pltpu.CompilerParams(dimension_semantics=("arbitrary",), vmem_limit_bytes=VMEM_LIMIT),
        name="moe",
    )(tile_group, src_row, dst_row, hp_all, comb_sorted, g_ffn, w1g, w3g, w2g)


def kernel(x_prompt, x_sample, cache_ckv, cache_krope, cache_dk, cache_dv, page_table, meta_tokens, g_attn, w_in, g_cq, g_ckv, w_uq, g_qn_mla, w_ukv, g_kn_mla, g_o_mla, g_qn_diff, g_kn_diff, lambda_q1, lambda_k1, lambda_q2, lambda_k2, g_subln, w_o, g_ffn, w_gate_group, b_gate_group, w_gate_expert, b_gate_expert, w1, w3, w2):
    assert g_attn.shape[0] == 1, "single layer step"
    n_b, seq, d = x_prompt.shape
    n_seq, n_new, _ = x_sample.shape
    n_meta = meta_tokens.shape[0]
    past_len = page_table.shape[1] * PAGE
    lam_init = 0.8 - 0.6 * math.exp(-0.3 * 0)

    w_in_p, wq_p, wkv_p = _prep_proj_weights(w_in[0], w_uq[0], w_ukv[0])
    pconsts = (g_attn, w_in_p, g_cq, g_ckv, wq_p, wkv_p, _seg_ones(LANES), _seg_ones(DIFF_HD))
    gains = (g_qn_mla[0], g_kn_mla[0], g_qn_diff[0], g_kn_diff[0])
    tab_meta = _tables(jnp.arange(n_meta), *gains)
    tab_prompt = _tables(n_meta + jnp.arange(seq), *gains)
    tab_new = jnp.tile(_tables(past_len + jnp.arange(n_new), *gains), (PAGE // n_new, 1))

    tm = min(256, seq)
    meta = _project(meta_tokens.astype(F32), tab_meta, 1, n_meta, pconsts)
    main = _project(x_prompt.reshape(n_b * seq, d), tab_prompt, n_b, tm, pconsts)
    new = _project(x_sample.reshape(n_seq * n_new, d), tab_new, (n_seq * n_new) // PAGE, PAGE, pconsts)

    lam = jnp.concatenate([lambda_q1, lambda_k1, lambda_q2, lambda_k2, jnp.full((1, DIFF_HD), lam_init, F32)], axis=0)
    g_o = jnp.tile(g_o_mla, (1, LANES // MLA_V))
    tq = min(512, seq)
    oa_p = _prompt_attn(main[4], main[5], main[6], meta[5], meta[6], g_o, lam, n_b, tq, True, 1.0)
    ob_p = _prompt_attn(main[7], main[8], main[9], meta[8], meta[9], g_subln, lam, n_b, tq, False, 1.0 - lam_init)

    oa_s, ob_s = _decode_attn(new, cache_ckv[0], cache_krope[0], cache_dk[0], cache_dv[0], page_table,
                              w_ukv[0], gains, g_o, g_subln, lam, lam_init, n_seq, n_new)

    wo = w_o[0].astype(BF16)
    n_a = MLA_HEADS * MLA_V
    wg = jnp.concatenate([w_gate_group[0].T, jnp.zeros((8 - N_GROUPS, d), F32),
                          w_gate_expert[0].reshape(d, N_GROUPS * EXP_PER_GROUP).T], axis=0)
    wg_hi = wg.astype(BF16)
    wg_lo = (wg - wg_hi.astype(F32)).astype(BF16)
    bias = jnp.concatenate([b_gate_group[0], jnp.zeros((8 - N_GROUPS,), F32), b_gate_expert[0].reshape(-1)])[:, None]
    gconsts = (wo[:n_a], wo[n_a:], g_ffn, wg_hi, wg_lo, bias)
    n_p, n_s = n_b * seq, n_seq * n_new
    n_all = n_p + n_s
    hp_all, gsel, comb = _out_gate((x_prompt.reshape(n_p, d), oa_p, ob_p), (x_sample.reshape(n_s, d), oa_s, ob_s),
                                   gconsts)
    y_all = _moe(hp_all, n_all, gsel[0], comb, g_ffn, w1[0], w3[0], w2[0], 256)
    y_p, y_s = y_all[:n_p], y_all[n_p:n_all]

    def with_meta(m, x, tail):
        m = jnp.broadcast_to(m[None], (n_b,) + m.shape)
        return jnp.concatenate([m, x.reshape((n_b, seq) + m.shape[2:])], axis=1).reshape((1, n_b, n_meta + seq) + tail)

    return (y_p.reshape(n_b, seq, d), y_s.reshape(n_seq, n_new, d),
            with_meta(meta[0], main[0], (KV_LORA,)), with_meta(meta[1], main[1], (MLA_ROPE,)),
            with_meta(meta[2], main[2], (DIFF_MAPS, DIFF_HD)), with_meta(meta[3], main[3], (DIFF_HEADS, DIFF_V)),
            new[0].reshape(1, n_seq, n_new, KV_LORA), new[1].reshape(1, n_seq, n_new, MLA_ROPE),
            new[2].reshape(1, n_seq, n_new, DIFF_MAPS, DIFF_HD), new[3].reshape(1, n_seq, n_new, DIFF_HEADS, DIFF_V))
```

```python
import functools
import math

import jax
import jax.numpy as jnp
from jax import lax
from jax.experimental import pallas as pl
from jax.experimental.pallas import tpu as pltpu

F32 = jnp.float32
BF16 = jnp.bfloat16
I32 = jnp.int32

RMS_EPS = 1e-6
ROPE_THETA = 10000.0
NEG_INF = -1e30
PAGE = 128
LANES = 128

MLA_HEADS = 8
MLA_NOPE = 64
MLA_ROPE = 32
MLA_QK = MLA_NOPE + MLA_ROPE
MLA_V = 64
Q_LORA = 256
KV_LORA = 128
DIFF_HEADS = 4
DIFF_HD = 64
DIFF_V = 128
DIFF_MAPS = 2 * DIFF_HEADS
N_GROUPS = 4
EXP_PER_GROUP = 8
D_EXPERT = 256

VMEM_LIMIT = 56 * 1024 * 1024

ATTN_PAIRS = 2
N_TAB = 12


def _dot(a, b):
    return jnp.dot(a, b, preferred_element_type=F32)


def _dot_nt(a, b):
    return lax.dot_general(a, b, (((1,), (1,)), ((), ())), preferred_element_type=F32)


def _rope_cs(pos, d):
    inv = 1.0 / (ROPE_THETA ** (jnp.arange(0, d, 2, dtype=F32) / d))
    ang = pos.astype(F32)[:, None] * inv[None, :]
    return jnp.cos(ang), jnp.sin(ang)


def _mla_tables(pos, g, scale):
    c, s = _rope_cs(pos, MLA_ROPE)
    n = pos.shape[0]
    h = MLA_ROPE // 2
    gn, g1, g2 = g[:MLA_NOPE], g[MLA_NOPE:MLA_NOPE + h], g[MLA_NOPE + h:]
    z = lambda w: jnp.zeros((n, w), F32)
    big_c = jnp.concatenate([jnp.broadcast_to(gn[None], (n, MLA_NOPE)), c * g1, c * g2, z(LANES - MLA_QK)], 1)
    big_a = jnp.concatenate([z(MLA_NOPE), -s * g2, z(LANES - MLA_NOPE - h)], 1)
    big_b = jnp.concatenate([z(MLA_NOPE + h), s * g1, z(LANES - MLA_QK)], 1)
    return [big_c * scale, big_a * scale, big_b * scale]


def _diff_tables(pos, g, scale):
    c, s = _rope_cs(pos, DIFF_HD)
    n = pos.shape[0]
    h = DIFF_HD // 2
    z = jnp.zeros((n, h), F32)
    c64 = jnp.concatenate([c * g[:h], c * g[h:]], 1)
    a64 = jnp.concatenate([-s * g[h:], z], 1)
    b64 = jnp.concatenate([z, s * g[:h]], 1)
    return [jnp.tile(t, (1, 2)) * scale for t in (c64, a64, b64)]


def _tables(pos, g_qn_mla, g_kn_mla, g_qn_diff, g_kn_diff):
    log2e = math.log2(math.e)
    t = (_mla_tables(pos, g_qn_mla, MLA_QK ** -0.5 * log2e) + _mla_tables(pos, g_kn_mla, 1.0)
         + _diff_tables(pos, g_qn_diff, DIFF_HD ** -0.5 * log2e) + _diff_tables(pos, g_kn_diff, 1.0))
    return jnp.concatenate(t, axis=1)


def _prep_proj_weights(w_in, w_uq, w_ukv):
    d = w_in.shape[0]
    off_ckv, off_kr = Q_LORA, Q_LORA + KV_LORA
    off_rest = off_kr + MLA_ROPE
    kr = w_in[:, off_kr:off_rest]
    z = jnp.zeros((d, MLA_ROPE), w_in.dtype)
    w_in_p = jnp.concatenate([w_in[:, :off_kr], kr, z, kr, z, w_in[:, off_rest:]], axis=1).astype(BF16)
    wq = w_uq.reshape(Q_LORA, MLA_HEADS, MLA_QK)
    wq = jnp.pad(wq, ((0, 0), (0, 0), (0, LANES - MLA_QK))).reshape(Q_LORA, MLA_HEADS * LANES).astype(BF16)
    wkv = w_ukv.reshape(KV_LORA, MLA_HEADS, MLA_NOPE + MLA_V)
    wk = jnp.pad(wkv[..., :MLA_NOPE], ((0, 0), (0, 0), (0, LANES - MLA_NOPE))).reshape(KV_LORA, MLA_HEADS * LANES)
    wv = wkv[..., MLA_NOPE:].reshape(KV_LORA, MLA_HEADS * MLA_V)
    wkv_p = jnp.concatenate([wk, wv], axis=1).astype(BF16)
    return w_in_p, wq, wkv_p


def _seg_ones(width):
    i = jnp.arange(LANES)
    return (i[:, None] // width == i[None, :] // width).astype(BF16)


def _segsum(sq, e):
    hi = sq.astype(BF16)
    lo = (sq - hi.astype(F32)).astype(BF16)
    return _dot(hi, e) + _dot(lo, e)


def _norm_rope(x, tab_ref, t0, e, inv_n, sh_up, sh_dn):
    r = lax.rsqrt(_segsum(x * x, e) * inv_n + RMS_EPS)
    c = tab_ref[:, (t0 + 0) * LANES:(t0 + 1) * LANES]
    a = tab_ref[:, (t0 + 1) * LANES:(t0 + 2) * LANES]
    b = tab_ref[:, (t0 + 2) * LANES:(t0 + 3) * LANES]
    rot = x * c + pltpu.roll(x, sh_up, 1) * a + pltpu.roll(x, sh_dn, 1) * b
    return rot * r


def _proj_kernel(x_ref, tab_ref, gattn_ref, win_ref, gcq_ref, gckv_ref, wuq_ref, wkv_ref, e128_ref, e64_ref,
                 ckv_ref, kr_ref, dk32_ref, dv32_ref, qm_ref, km_ref, vm_ref, dq_ref, dk16_ref, dv16_ref):
    x = x_ref[...]
    u = x * lax.rsqrt(jnp.mean(x * x, axis=-1, keepdims=True) + RMS_EPS) * gattn_ref[...]
    proj = _dot(u.astype(BF16), win_ref[...])
    o_ckv, o_kr, o_dq = Q_LORA, Q_LORA + KV_LORA, Q_LORA + KV_LORA + LANES
    o_dk = o_dq + DIFF_MAPS * DIFF_HD
    o_dv = o_dk + DIFF_MAPS * DIFF_HD

    cq = proj[:, :Q_LORA]
    cqn = cq * lax.rsqrt(jnp.mean(cq * cq, axis=-1, keepdims=True) + RMS_EPS) * gcq_ref[...]
    ckv = proj[:, o_ckv:o_kr]
    ckvn = ckv * lax.rsqrt(jnp.mean(ckv * ckv, axis=-1, keepdims=True) + RMS_EPS) * gckv_ref[...]
    ckv_ref[...] = ckvn
    krs = proj[:, o_kr:o_dq]
    kr_ref[...] = krs[:, :MLA_ROPE]

    e128 = e128_ref[...]
    e64 = e64_ref[...]
    up, dn = LANES - MLA_ROPE // 2, MLA_ROPE // 2
    qm = _dot(cqn.astype(BF16), wuq_ref[...])
    for h in range(MLA_HEADS):
        sl = slice(h * LANES, (h + 1) * LANES)
        qm_ref[:, sl] = _norm_rope(qm[:, sl], tab_ref, 0, e128, 1.0 / MLA_QK, up, dn).astype(BF16)

    kv = _dot(ckvn.astype(BF16), wkv_ref[...])
    lane = lax.broadcasted_iota(I32, (1, LANES), 1)
    kr_mid = krs * ((lane >= MLA_NOPE) & (lane < MLA_QK)).astype(F32)
    for h in range(MLA_HEADS):
        sl = slice(h * LANES, (h + 1) * LANES)
        km_ref[:, sl] = _norm_rope(kv[:, sl] + kr_mid, tab_ref, 3, e128, 1.0 / MLA_QK, up, dn).astype(BF16)
    vm_ref[...] = kv[:, MLA_HEADS * LANES:].astype(BF16)

    up, dn = LANES - DIFF_HD // 2, DIFF_HD // 2
    for j in range(DIFF_MAPS * DIFF_HD // LANES):
        sl = slice(j * LANES, (j + 1) * LANES)
        dq_ref[:, sl] = _norm_rope(proj[:, o_dq + j * LANES:o_dq + (j + 1) * LANES], tab_ref, 6, e64,
                                   1.0 / DIFF_HD, up, dn).astype(BF16)
        dk = _norm_rope(proj[:, o_dk + j * LANES:o_dk + (j + 1) * LANES], tab_ref, 9, e64, 1.0 / DIFF_HD, up, dn)
        dk32_ref[:, sl] = dk
        dk16_ref[:, sl] = dk.astype(BF16)
    dv = proj[:, o_dv:]
    dv32_ref[...] = dv
    dv16_ref[...] = dv.astype(BF16)


def _project(x, tab, n_batch, tm, consts):
    t, d = x.shape
    n_i = t // n_batch // tm
    row = lambda b, i: (b * n_i + i, 0)
    fixed = lambda b, i: (0, 0)
    gattn, win, gcq, gckv, wuq, wkv, e128, e64 = consts
    widths = [(KV_LORA, F32), (MLA_ROPE, F32), (DIFF_MAPS * DIFF_HD, F32), (DIFF_HEADS * DIFF_V, F32),
              (MLA_HEADS * LANES, BF16), (MLA_HEADS * LANES, BF16), (MLA_HEADS * MLA_V, BF16),
              (DIFF_MAPS * DIFF_HD, BF16), (DIFF_MAPS * DIFF_HD, BF16), (DIFF_HEADS * DIFF_V, BF16)]
    return pl.pallas_call(
        _proj_kernel,
        grid=(n_batch, n_i),
        in_specs=[pl.BlockSpec((tm, d), row), pl.BlockSpec((tm, N_TAB * LANES), lambda b, i: (i, 0))]
        + [pl.BlockSpec(c.shape, fixed) for c in consts],
        out_specs=[pl.BlockSpec((tm, w), row) for w, _ in widths],
        out_shape=[jax.ShapeDtypeStruct((t, w), dt) for w, dt in widths],
        compiler_params=pltpu.CompilerParams(dimension_semantics=("arbitrary", "arbitrary"),
                                             vmem_limit_bytes=VMEM_LIMIT),
        name="project",
    )(x, tab, *consts)


def _lane_tile(x, width):
    if width <= LANES:
        return x[:, :width]
    return jnp.concatenate([x] * (width // LANES), axis=1)


def _flash_update(a, q, k, v, m_scr, l_scr, acc_scr):
    s = _dot_nt(q, k)
    m_prev = m_scr[a]
    m_next = jnp.maximum(m_prev, jnp.max(s, axis=1, keepdims=True))
    alpha = jnp.exp2(m_prev - m_next)
    p = jnp.exp2(s - _lane_tile(m_next, s.shape[1]))
    l_scr[a] = alpha * l_scr[a] + jnp.sum(p, axis=1, keepdims=True)
    acc_scr[a] = acc_scr[a] * alpha + _dot(p.astype(BF16), v)
    m_scr[a] = m_next


def _pattn_kernel(q_ref, k_ref, v_ref, kmeta_ref, vmeta_ref, gain_ref, lam_ref, o_ref, m_scr, l_scr, acc_scr,
                  *, mla, tq, n_pair, n_meta, post_scale):
    i = pl.program_id(2)
    lane = lax.broadcasted_iota(I32, (1, LANES), 1)
    lo_half = lane < LANES // 2
    slab = lambda j: slice(j * LANES, (j + 1) * LANES)
    maps = range(2 * n_pair)
    if mla:
        qs = [q_ref[:, slab(a)] for a in maps]
        ksl = [slab(a) for a in maps]
    else:
        half = [lo_half.astype(BF16), (1 - lo_half.astype(I32)).astype(BF16)]
        qs = [q_ref[:, slab(a // 2)] * half[a % 2] for a in maps]
        ksl = [slab(a // 2) for a in maps]

    k0 = pl.multiple_of(i * tq, tq)
    causal = lax.broadcasted_iota(I32, (tq, tq), 0) >= lax.broadcasted_iota(I32, (tq, tq), 1)
    meta_ok = lane < n_meta
    for a in maps:
        s1 = jnp.where(causal, _dot_nt(qs[a], k_ref[pl.ds(k0, tq), ksl[a]]), NEG_INF)
        s2 = jnp.where(meta_ok, _dot_nt(qs[a], kmeta_ref[:, ksl[a]]), NEG_INF)
        m = jnp.maximum(jnp.max(s1, axis=1, keepdims=True), jnp.max(s2, axis=1, keepdims=True))
        p1 = jnp.exp2(s1 - m)
        p2 = jnp.exp2(s2 - m)
        m_scr[a] = jnp.broadcast_to(m, (tq, LANES))
        l_scr[a] = jnp.broadcast_to(jnp.sum(p1, axis=1, keepdims=True) + jnp.sum(p2, axis=1, keepdims=True),
                                    (tq, LANES))
        acc_scr[a] = (_dot(p1.astype(BF16), v_ref[pl.ds(k0, tq), slab(a // 2)])
                      + _dot(p2.astype(BF16), vmeta_ref[:, slab(a // 2)]))

    def body(j, carry):
        kj = pl.multiple_of(j * tq, tq)
        for a in maps:
            _flash_update(a, qs[a], k_ref[pl.ds(kj, tq), ksl[a]], v_ref[pl.ds(kj, tq), slab(a // 2)],
                          m_scr, l_scr, acc_scr)
        return carry

    lax.fori_loop(0, i, body, 0)

    for pr in range(n_pair):
        o0 = acc_scr[2 * pr] / l_scr[2 * pr]
        o1 = acc_scr[2 * pr + 1] / l_scr[2 * pr + 1]
        if mla:
            o = jnp.where(lo_half, o0, o1)
            sq = o * o
            ss_lo = jnp.sum(jnp.where(lo_half, sq, 0.0), axis=1, keepdims=True)
            ss_hi = jnp.sum(jnp.where(lo_half, 0.0, sq), axis=1, keepdims=True)
            r = jnp.where(lo_half, lax.rsqrt(ss_lo * (2.0 / LANES) + RMS_EPS),
                          lax.rsqrt(ss_hi * (2.0 / LANES) + RMS_EPS))
        else:
            o = o0 - _lambda(lam_ref) * o1
            r = lax.rsqrt(jnp.mean(o * o, axis=1, keepdims=True) + RMS_EPS)
        o_ref[:, slab(pr)] = (o * r * gain_ref[...] * post_scale).astype(o_ref.dtype)


def _lambda(lam_ref):
    lam = lam_ref[...]
    a = jnp.exp(jnp.sum(lam[0:1] * lam[1:2], axis=1, keepdims=True))
    b = jnp.exp(jnp.sum(lam[2:3] * lam[3:4], axis=1, keepdims=True))
    return a - b + lam[4:5, 0:1]


def _prompt_attn(q, k, v, kmeta, vmeta, gain, lam, n_batch, tq, mla, n_pair, post_scale):
    t = q.shape[0]
    s = t // n_batch
    n_q = s // tq
    n_meta = kmeta.shape[0]
    assert n_meta <= LANES
    kmeta = jnp.pad(kmeta, ((0, LANES - n_meta), (0, 0)))
    vmeta = jnp.pad(vmeta, ((0, LANES - n_meta), (0, 0)))
    kw = (2 * LANES if mla else LANES) * n_pair
    vw = LANES * n_pair
    n_hp = q.shape[1] // kw
    kern = functools.partial(_pattn_kernel, mla=mla, tq=tq, n_pair=n_pair, n_meta=n_meta, post_scale=post_scale)
    return pl.pallas_call(
        kern,
        grid=(n_batch, n_hp, n_q),
        in_specs=[pl.BlockSpec((tq, kw), lambda b, h, i: (b * n_q + i, h)),
                  pl.BlockSpec((s, kw), lambda b, h, i: (b, h)),
                  pl.BlockSpec((s, vw), lambda b, h, i: (b, h)),
                  pl.BlockSpec((LANES, kw), lambda b, h, i: (0, h)),
                  pl.BlockSpec((LANES, vw), lambda b, h, i: (0, h)),
                  pl.BlockSpec((1, LANES), lambda b, h, i: (0, 0)),
                  pl.BlockSpec(lam.shape, lambda b, h, i: (0, 0))],
        out_specs=pl.BlockSpec((tq, vw), lambda b, h, i: (b * n_q + i, h)),
        out_shape=jax.ShapeDtypeStruct((t, n_hp * vw), BF16),
        scratch_shapes=[pltpu.VMEM((2 * n_pair, tq, LANES), F32)] * 3,
        compiler_params=pltpu.CompilerParams(dimension_semantics=("arbitrary",) * 3, vmem_limit_bytes=VMEM_LIMIT),
        name="prompt_attn_mla" if mla else "prompt_attn_diff",
    )(q, k, v, kmeta, vmeta, gain, lam)


Q_ABS_W = 2 * LANES
PAGES_PER_STEP = 32


def _absorb_weights(w_ukv, g_kn_mla):
    wk = w_ukv.reshape(KV_LORA, MLA_HEADS, MLA_NOPE + MLA_V)[..., :MLA_NOPE]
    h = MLA_ROPE // 2
    gn, g1, g2 = g_kn_mla[:MLA_NOPE], g_kn_mla[MLA_NOPE:MLA_NOPE + h], g_kn_mla[MLA_NOPE + h:]
    m = jnp.zeros((MLA_HEADS, LANES, Q_ABS_W), F32)
    m = m.at[:, :MLA_NOPE, :KV_LORA].set(jnp.transpose(wk, (1, 2, 0)) * gn[None, :, None])
    i = jnp.arange(h)
    m = m.at[:, MLA_NOPE + i, LANES + i].set(g1)
    m = m.at[:, MLA_NOPE + h + i, LANES + h + i].set(g2)
    m = m.at[:, MLA_NOPE + h + i, LANES + MLA_ROPE + i].set(g1)
    m = m.at[:, MLA_NOPE + i, LANES + MLA_ROPE + h + i].set(-g2)
    return m.astype(BF16)


def _absorb_kernel(q_ref, m_ref, o_ref):
    n = q_ref.shape[0] // 8
    for h in range(MLA_HEADS):
        o_ref[:, h, :, :] = _dot(q_ref[:, h * LANES:(h + 1) * LANES], m_ref[h]).reshape(n, 8, Q_ABS_W)


def _absorb(qm, m_abs, n_new):
    t = qm.shape[0]
    tm = PAGE
    assert n_new == 8 and t % tm == 0
    return pl.pallas_call(
        _absorb_kernel,
        grid=(t // tm,),
        in_specs=[pl.BlockSpec((tm, MLA_HEADS * LANES), lambda i: (i, 0)),
                  pl.BlockSpec(m_abs.shape, lambda i: (0, 0, 0))],
        out_specs=pl.BlockSpec((tm // n_new, MLA_HEADS, n_new, Q_ABS_W), lambda i: (i, 0, 0, 0)),
        out_shape=jax.ShapeDtypeStruct((t // n_new, MLA_HEADS, n_new, Q_ABS_W), F32),
        compiler_params=pltpu.CompilerParams(dimension_semantics=("arbitrary",), vmem_limit_bytes=VMEM_LIMIT),
        name="absorb_q",
    )(qm, m_abs)


def _softmax_step(s, m_scr, l_scr, acc_scr):
    m_prev = m_scr[...]
    m_next = jnp.maximum(m_prev, jnp.max(s, axis=1, keepdims=True))
    alpha = jnp.exp2(m_prev - m_next)
    w = s.shape[1]
    p = jnp.exp2(s - _lane_tile(m_next, w))
    l_scr[...] = alpha * l_scr[...] + jnp.sum(p, axis=1, keepdims=True)
    acc_scr[...] = acc_scr[...] * alpha
    m_scr[...] = m_next
    return p


def _decode_kernel(pt_ref, qabs_ref, qrow_ref, knew_ref, vnew_ref, qd_ref, kdnew_ref, vdnew_ref, tabc_ref, tabs_ref,
                   wukt_ref, wuv_ref, go_ref, gsub_ref, lam_ref, ckv_hbm, kr_hbm, dk_hbm, dv_hbm, oa_ref, ob_ref,
                   ma, la, acca, md, ld, accd, sa, sd, ckv_buf, kr_buf, dk_buf, dv_buf, sems, *, n_pg, post_scale):
    c = pl.program_id(1)
    n_chunks = pl.num_programs(1)
    step = pl.program_id(0) * n_chunks + c
    last_step = pl.num_programs(0) * n_chunks - 1
    slot = step % 2
    other = 1 - slot
    nxt = jnp.minimum(step + 1, last_step)
    n_rows = MLA_HEADS * 8
    pools = ((ckv_hbm, ckv_buf), (kr_hbm, kr_buf), (dk_hbm, dk_buf), (dv_hbm, dv_buf))

    def fetch_page(step_idx, buf_slot, p):
        page = pt_ref[step_idx * n_pg + p]
        for k, (pool, buf) in enumerate(pools):
            pltpu.make_async_copy(pool.at[page], buf.at[buf_slot, p], sems.at[k, buf_slot]).start()

    def wait_pages(buf_slot):
        for k, (pool, buf) in enumerate(pools):
            pltpu.make_async_copy(pool.at[pl.ds(0, n_pg)], buf.at[buf_slot], sems.at[k, buf_slot]).wait()

    @pl.when(step == 0)
    def _():
        for p in range(n_pg):
            fetch_page(0, 0, p)

    @pl.when(c == 0)
    def _():
        for m_scr, l_scr, acc_scr in ((ma, la, acca), (md, ld, accd)):
            m_scr[...] = jnp.full(m_scr.shape, NEG_INF, F32)
            l_scr[...] = jnp.zeros(l_scr.shape, F32)
            acc_scr[...] = jnp.zeros(acc_scr.shape, F32)

    wait_pages(slot)
    qabs = qabs_ref[...].reshape(n_rows, Q_ABS_W)
    lhs_lat = jnp.concatenate([qabs[:, :KV_LORA], wukt_ref[...]], axis=0)
    q_cs = qabs[:, LANES:LANES + 2 * MLA_ROPE]
    qd = qd_ref[...]
    ri = lax.broadcasted_iota(I32, (n_rows, DIFF_MAPS * DIFF_HD), 0)
    ci = lax.broadcasted_iota(I32, (n_rows, DIFF_MAPS * DIFF_HD), 1)
    q_bd = jnp.where(ri // 8 == ci // DIFF_HD, jnp.concatenate([qd] * DIFF_MAPS, axis=1), 0.0)

    for p in range(n_pg):
        cols = slice(p * PAGE, (p + 1) * PAGE)
        fetch_page(nxt, other, p)
        kr_t = kr_buf[slot, p]
        big = _dot_nt(lhs_lat, ckv_buf[slot, p])
        kn_t = big[n_rows:]
        ss = jnp.sum((kn_t * kn_t).reshape(MLA_HEADS, MLA_NOPE, PAGE), axis=1)
        ss = ss + jnp.sum(kr_t * kr_t, axis=0, keepdims=True)
        r_t = lax.rsqrt(ss * (1.0 / MLA_QK) + RMS_EPS)
        f_cs = jnp.concatenate([kr_t * tabc_ref[:, cols], kr_t * tabs_ref[:, cols]], axis=0)
        s_pg = big[:n_rows] + _dot(q_cs, f_cs)
        for h in range(MLA_HEADS):
            sa[h * 8:(h + 1) * 8, cols] = s_pg[h * 8:(h + 1) * 8, :] * r_t[h:h + 1, :]
        sd[:, cols] = _dot(q_bd, dk_buf[slot, p].reshape(DIFF_MAPS * DIFF_HD, PAGE))

    pa = _softmax_step(sa[...], ma, la, acca)
    pd = _softmax_step(sd[...], md, ld, accd)
    acc_a = acca[...]
    acc_d = accd[...]
    head_of_row = lax.broadcasted_iota(I32, (n_rows, PAGE), 0) // 16
    for p in range(n_pg):
        cols = slice(p * PAGE, (p + 1) * PAGE)
        acc_a = acc_a + _dot(pa[:, cols], ckv_buf[slot, p])
        v_stack = jnp.concatenate([dv_buf[slot, p, pl.ds(h, PAGE, stride=DIFF_HEADS), :]
                                   for h in range(DIFF_HEADS)], axis=0)
        pd_pg = pd[:, cols]
        p_wide = jnp.concatenate([jnp.where(head_of_row == h, pd_pg, 0.0) for h in range(DIFF_HEADS)], axis=1)
        acc_d = acc_d + _dot(p_wide, v_stack)
    acca[...] = acc_a
    accd[...] = acc_d

    @pl.when(step == last_step)
    def _():
        wait_pages(other)

    @pl.when(c == pl.num_programs(1) - 1)
    def _():
        ri = lax.broadcasted_iota(I32, (n_rows, n_rows), 0)
        ci = lax.broadcasted_iota(I32, (n_rows, n_rows), 1)
        valid = (ri // 8 == ci // 8) & (ci % 8 <= ri % 8)

        s_new = jnp.where(valid, _dot_nt(qrow_ref[...], knew_ref[...]), NEG_INF)
        p_new = _softmax_step(s_new, ma, la, acca)
        o_new = _dot(p_new, vnew_ref[...])
        lat = acca[...]
        lat_hi = lat.astype(BF16).astype(F32)
        lat_lo = lat - lat_hi
        l_a = la[...]
        for h in range(MLA_HEADS):
            rows = slice(h * 8, (h + 1) * 8)
            wv = wuv_ref[h]
            o = (_dot(lat_hi[rows], wv) + _dot(lat_lo[rows], wv) + o_new[rows]) / l_a[rows, :MLA_V]
            r = lax.rsqrt(jnp.mean(o * o, axis=1, keepdims=True) + RMS_EPS)
            oa_ref[rows, :] = (o * r * go_ref[...]).astype(oa_ref.dtype)

        sd_new = jnp.where(valid, _dot_nt(qd, kdnew_ref[...]), NEG_INF)
        pd_new = _softmax_step(sd_new, md, ld, accd)
        od = (accd[...] + _dot(pd_new, vdnew_ref[...])) / ld[...]
        lam = _lambda(lam_ref)
        for h in range(DIFF_HEADS):
            o = od[2 * h * 8:(2 * h + 1) * 8] - lam * od[(2 * h + 1) * 8:(2 * h + 2) * 8]
            r = lax.rsqrt(jnp.mean(o * o, axis=1, keepdims=True) + RMS_EPS)
            ob_ref[h * 8:(h + 1) * 8, :] = (o * r * gsub_ref[...] * post_scale).astype(ob_ref.dtype)


def _decode_attn(new, pool_ckv, pool_kr, pool_dk, pool_dv, page_table, w_ukv, gains, g_o, g_subln, lam, lam_init,
                 n_seq, n_new):
    assert n_new == 8
    n_pages = page_table.shape[1]
    n_pg = min(PAGES_PER_STEP, n_pages)
    assert n_pages % n_pg == 0
    past_len = n_pages * PAGE
    qm, km, vm, dq, dk16, dv16 = new[4:10]

    def rows_by_head(x, n_h):
        w = x.shape[1] // n_h
        x = jnp.transpose(x.reshape(n_seq, n_new, n_h, w), (0, 2, 1, 3))
        return x.reshape(n_seq, n_h * n_new, w).astype(F32)

    qabs = _absorb(qm, _absorb_weights(w_ukv, gains[1]), n_new)
    qrow = rows_by_head(qm, MLA_HEADS)
    knew = rows_by_head(km, MLA_HEADS)
    vnew = rows_by_head(vm, MLA_HEADS)
    qd = rows_by_head(dq, DIFF_MAPS)
    kdnew = rows_by_head(dk16, DIFF_MAPS)
    vdnew = jnp.repeat(rows_by_head(dv16, DIFF_HEADS).reshape(n_seq, DIFF_HEADS, n_new, DIFF_V), 2, axis=1)
    vdnew = vdnew.reshape(n_seq, DIFF_MAPS * n_new, DIFF_V)

    pool_kr = jnp.transpose(pool_kr, (0, 2, 1))
    pool_dk = jnp.transpose(pool_dk, (0, 2, 3, 1))
    pool_dv = pool_dv.reshape(pool_dv.shape[0], PAGE * DIFF_HEADS, DIFF_V)
    cs, sn = _rope_cs(jnp.arange(past_len), MLA_ROPE)
    tabc = jnp.concatenate([cs, cs], axis=1).T
    tabs = jnp.concatenate([sn, sn], axis=1).T
    wkv = w_ukv.reshape(KV_LORA, MLA_HEADS, MLA_NOPE + MLA_V)
    wukt = wkv[..., :MLA_NOPE].reshape(KV_LORA, MLA_HEADS * MLA_NOPE).T
    wuv = jnp.transpose(wkv[..., MLA_NOPE:], (1, 0, 2))

    per_seq = lambda a: pl.BlockSpec((None,) + a.shape[1:], lambda s, c, pt: (s,) + (0,) * (a.ndim - 1))
    fixed = lambda a: pl.BlockSpec(a.shape, lambda s, c, pt: (0,) * a.ndim)
    tab_spec = pl.BlockSpec((MLA_ROPE, n_pg * PAGE), lambda s, c, pt: (0, c))

    seq_in = (qabs, qrow, knew, vnew, qd, kdnew, vdnew)
    consts = (wukt, wuv, g_o[:, :MLA_V], g_subln, lam)
    pools = (pool_ckv, pool_kr, pool_dk, pool_dv)
    n_rows = MLA_HEADS * n_new
    grid_spec = pltpu.PrefetchScalarGridSpec(
        num_scalar_prefetch=1,
        grid=(n_seq, n_pages // n_pg),
        in_specs=[per_seq(a) for a in seq_in] + [tab_spec, tab_spec] + [fixed(a) for a in consts]
        + [pl.BlockSpec(memory_space=pl.ANY)] * len(pools),
        out_specs=[pl.BlockSpec((None, n_rows, MLA_V), lambda s, c, pt: (s, 0, 0)),
                   pl.BlockSpec((None, DIFF_HEADS * n_new, DIFF_V), lambda s, c, pt: (s, 0, 0))],
        scratch_shapes=[pltpu.VMEM((n_rows, LANES), F32)] * 6
        + [pltpu.VMEM((n_rows, n_pg * PAGE), F32)] * 2
        + [pltpu.VMEM((2, n_pg) + pool.shape[1:], F32) for pool in pools]
        + [pltpu.SemaphoreType.DMA((len(pools), 2))],
    )
    oa, ob = pl.pallas_call(
        functools.partial(_decode_kernel, n_pg=n_pg, post_scale=1.0 - lam_init),
        grid_spec=grid_spec,
        out_shape=[jax.ShapeDtypeStruct((n_seq, n_rows, MLA_V), BF16),
                   jax.ShapeDtypeStruct((n_seq, DIFF_HEADS * n_new, DIFF_V), BF16)],
        compiler_params=pltpu.CompilerParams(dimension_semantics=("arbitrary", "arbitrary"),
                                             vmem_limit_bytes=VMEM_LIMIT),
        name="decode_attn",
    )(page_table.reshape(-1), *seq_in, tabc, tabs, *consts, *pools)

    def tokens_major(o, n_h):
        w = o.shape[2]
        return jnp.transpose(o.reshape(n_seq, n_h, n_new, w), (0, 2, 1, 3)).reshape(n_seq * n_new, n_h * w)

    return tokens_major(oa, MLA_HEADS), tokens_major(ob, DIFF_HEADS)


def _split_bf16(x):
    hi = x.astype(BF16)
    return hi, (x - hi.astype(F32)).astype(BF16)


def _out_gate_kernel(x_ref, oa_ref, ob_ref, woa_ref, wob_ref, gffn_ref, wg_hi_ref, wg_lo_ref, bias_ref,
                     hp_ref, gsel_ref, comb_ref):
    hp = x_ref[...] + _dot(oa_ref[...], woa_ref[...]) + _dot(ob_ref[...], wob_ref[...])
    hp_ref[...] = hp
    h = hp * lax.rsqrt(jnp.mean(hp * hp, axis=-1, keepdims=True) + RMS_EPS) * gffn_ref[...]
    h_hi, h_lo = _split_bf16(h)
    w_hi = wg_hi_ref[...]
    lt = _dot_nt(w_hi, h_hi) + _dot_nt(w_hi, h_lo) + _dot_nt(wg_lo_ref[...], h_hi) + bias_ref[...]

    gl = [lt[g:g + 1, :] for g in range(N_GROUPS)]
    gmax = functools.reduce(jnp.maximum, gl)
    gsel = jnp.full(gmax.shape, N_GROUPS - 1, I32)
    for g in range(N_GROUPS - 2, -1, -1):
        gsel = jnp.where(gl[g] == gmax, g, gsel)
    gw = 1.0 / functools.reduce(lambda a, b: a + b, [jnp.exp(x - gmax) for x in gl])

    base = 8
    el = []
    for e in range(EXP_PER_GROUP):
        v = lt[base + (N_GROUPS - 1) * EXP_PER_GROUP + e:base + (N_GROUPS - 1) * EXP_PER_GROUP + e + 1, :]
        for g in range(N_GROUPS - 2, -1, -1):
            v = jnp.where(gsel == g, lt[base + g * EXP_PER_GROUP + e:base + g * EXP_PER_GROUP + e + 1, :], v)
        el.append(v)
    v1 = functools.reduce(jnp.maximum, el)
    i1 = jnp.full(v1.shape, EXP_PER_GROUP - 1, I32)
    for e in range(EXP_PER_GROUP - 2, -1, -1):
        i1 = jnp.where(el[e] == v1, e, i1)
    rest = [jnp.where(i1 == e, -jnp.inf, el[e]) for e in range(EXP_PER_GROUP)]
    v2 = functools.reduce(jnp.maximum, rest)
    i2 = jnp.full(v1.shape, EXP_PER_GROUP - 1, I32)
    for e in range(EXP_PER_GROUP - 2, -1, -1):
        i2 = jnp.where(rest[e] == v2, e, i2)
    e2 = jnp.exp(v2 - v1)
    w1 = gw / (1.0 + e2)
    w2 = gw * e2 / (1.0 + e2)
    gsel_ref[...] = gsel
    for e in range(EXP_PER_GROUP):
        comb_ref[e:e + 1, :] = jnp.where(i1 == e, w1, jnp.where(i2 == e, w2, 0.0))


def _out_gate_both_kernel(xp_ref, oap_ref, obp_ref, xs_ref, oas_ref, obs_ref, *rest, n_prompt_tiles):
    i = pl.program_id(0)

    @pl.when(i < n_prompt_tiles)
    def _():
        _out_gate_kernel(xp_ref, oap_ref, obp_ref, *rest)

    @pl.when(i >= n_prompt_tiles)
    def _():
        _out_gate_kernel(xs_ref, oas_ref, obs_ref, *rest)


def _out_gate(prompt, sample, consts):
    (xp, oap, obp), (xs, oas, obs) = prompt, sample
    (n_p, d), n_s = xp.shape, xs.shape[0]
    tm = math.gcd(256, n_s)
    assert n_p % tm == 0
    npt, n_t = n_p // tm, (n_p + n_s) // tm
    p_row = lambda i: (jnp.minimum(i, npt - 1), 0)
    s_row = lambda i: (jnp.maximum(i - npt, 0), 0)
    fixed = lambda i: (0, 0)
    specs = lambda arrs, idx: [pl.BlockSpec((tm, a.shape[1]), idx) for a in arrs]
    return pl.pallas_call(
        functools.partial(_out_gate_both_kernel, n_prompt_tiles=npt),
        grid=(n_t,),
        in_specs=specs(prompt, p_row) + specs(sample, s_row) + [pl.BlockSpec(c.shape, fixed) for c in consts],
        out_specs=[pl.BlockSpec((tm, d), lambda i: (i, 0)), pl.BlockSpec((1, tm), lambda i: (0, i)),
                   pl.BlockSpec((EXP_PER_GROUP, tm), lambda i: (0, i))],
        out_shape=[jax.ShapeDtypeStruct((n_p + n_s, d), F32), jax.ShapeDtypeStruct((1, n_p + n_s), I32),
                   jax.ShapeDtypeStruct((EXP_PER_GROUP, n_p + n_s), F32)],
        compiler_params=pltpu.CompilerParams(dimension_semantics=("arbitrary",), vmem_limit_bytes=VMEM_LIMIT),
        name="out_gate",
    )(xp, oap, obp, xs, oas, obs, *consts)


def _moe_kernel(tg_ref, src_ref, dst_ref, hp_ref, comb_ref, gffn_ref, w1_ref, w3_ref, w2_ref,
                y_ref, xbuf, obuf, sem_in, sem_out, *, tmoe):
    t = pl.program_id(0)
    n_t = pl.num_programs(0)
    slot = t % 2
    other = 1 - slot
    nxt = jnp.minimum(t + 1, n_t - 1)
    chunk = tmoe // EXP_PER_GROUP

    def gather(tile, buf_slot, r):
        row = src_ref[tile * tmoe + r]
        pltpu.make_async_copy(hp_ref.at[pl.ds(row, 1)], xbuf.at[buf_slot, pl.ds(r, 1)], sem_in.at[buf_slot]).start()

    def scatter(tile_shifted, buf_slot, r):
        row = dst_ref[tile_shifted * tmoe + r]
        pltpu.make_async_copy(obuf.at[buf_slot, pl.ds(r, 1)], y_ref.at[pl.ds(row, 1)], sem_out.at[buf_slot]).start()

    def wait_gathers(buf_slot):
        pltpu.make_async_copy(hp_ref.at[pl.ds(0, tmoe)], xbuf.at[buf_slot], sem_in.at[buf_slot]).wait()

    def wait_scatters(buf_slot):
        pltpu.make_async_copy(obuf.at[buf_slot], y_ref.at[pl.ds(0, tmoe)], sem_out.at[buf_slot]).wait()

    @pl.when(t == 0)
    def _():
        obuf[...] = jnp.zeros_like(obuf)

        def first(r, c):
            gather(0, 0, r)
            return c

        lax.fori_loop(0, tmoe, first, 0, unroll=8)

    wait_gathers(slot)
    x = xbuf[slot]
    h = (x * lax.rsqrt(jnp.mean(x * x, axis=-1, keepdims=True) + RMS_EPS) * gffn_ref[...]).astype(BF16)
    comb = comb_ref[...]
    y = x
    for e in range(EXP_PER_GROUP):
        for r in range(e * chunk, (e + 1) * chunk):
            gather(nxt, other, r)
            scatter(t, other, r)
        a = _dot(h, w1_ref[e])
        b = _dot(h, w3_ref[e])
        hid = a * (1.0 / (1.0 + jnp.exp(-a))) * b * comb[:, e:e + 1]
        y = y + _dot(hid.astype(BF16), w2_ref[e])
    wait_scatters(other)
    obuf[slot] = y

    @pl.when(t == n_t - 1)
    def _():
        def last(r, c):
            scatter(t + 1, slot, r)
            return c

        lax.fori_loop(0, tmoe, last, 0, unroll=8)
        wait_scatters(slot)
        wait_gathers(other)


def _moe(hp_all, n_all, gsel, comb, g_ffn, w1, w3, w2, tmoe):
    d = hp_all.shape[1]
    n_tiles = n_all // tmoe + N_GROUPS
    onehot = (gsel[:, None] == jnp.arange(N_GROUPS, dtype=I32)[None, :]).astype(I32)
    csum = jnp.cumsum(onehot, axis=0)
    rank = jnp.take_along_axis(csum, gsel[:, None], axis=1)[:, 0] - 1
    g_tiles = (csum[-1] + tmoe - 1) // tmoe
    g_first = jnp.cumsum(g_tiles) - g_tiles
    dest = g_first[gsel] * tmoe + rank
    row_tok = jnp.full((n_tiles * tmoe,), -1, I32).at[dest].set(jnp.arange(n_all, dtype=I32))
    tile_group = jnp.sum(jnp.arange(n_tiles, dtype=I32)[:, None] >= g_first[None, 1:], axis=1).astype(I32)
    comb_sorted = jnp.where(row_tok[:, None] >= 0, comb.T[jnp.maximum(row_tok, 0)], 0.0)
    spare = n_all + jnp.arange(tmoe, dtype=I32)
    src_row = jnp.maximum(row_tok, 0)
    dst_row = jnp.where(row_tok >= 0, row_tok, jnp.tile(spare, n_tiles))
    dst_row = jnp.concatenate([spare, dst_row])

    w1g, w3g, w2g = w1.astype(BF16), w3.astype(BF16), w2.astype(BF16)
    group_w = lambda w: pl.BlockSpec((None,) + w.shape[1:], lambda t, tg, src, dst: (tg[t], 0, 0, 0))

    grid_spec = pltpu.PrefetchScalarGridSpec(
        num_scalar_prefetch=3,
        grid=(n_tiles,),
        in_specs=[pl.BlockSpec(memory_space=pl.ANY),
                  pl.BlockSpec((tmoe, EXP_PER_GROUP), lambda t, tg, src, dst: (t, 0)),
                  pl.BlockSpec((1, d), lambda t, tg, src, dst: (0, 0)),
                  group_w(w1g), group_w(w3g), group_w(w2g)],
        out_specs=pl.BlockSpec(memory_space=pl.ANY),
        scratch_shapes=[pltpu.VMEM((2, tmoe, d), F32), pltpu.VMEM((2, tmoe, d), F32),
                        pltpu.SemaphoreType.DMA((2,)), pltpu.SemaphoreType.DMA((2,))],
    )
    return pl.pallas_call(
        functools.partial(_moe_kernel, tmoe=tmoe),
        grid_spec=grid_spec,
        out_shape=jax.ShapeDtypeStruct((n_all + tmoe, d), F32),
        compiler_params=pltpu.CompilerParams(dimension_semantics=("arbitrary",), vmem_limit_bytes=VMEM_LIMIT),
        name="moe",
    )(tile_group, src_row, dst_row, hp_all, comb_sorted, g_ffn, w1g, w3g, w2g)


def kernel(x_prompt, x_sample, cache_ckv, cache_krope, cache_dk, cache_dv, page_table, meta_tokens, g_attn, w_in, g_cq, g_ckv, w_uq, g_qn_mla, w_ukv, g_kn_mla, g_o_mla, g_qn_diff, g_kn_diff, lambda_q1, lambda_k1, lambda_q2, lambda_k2, g_subln, w_o, g_ffn, w_gate_group, b_gate_group, w_gate_expert, b_gate_expert, w1, w3, w2):
    assert g_attn.shape[0] == 1, "single layer step"
    n_b, seq, d = x_prompt.shape
    n_seq, n_new, _ = x_sample.shape
    n_meta = meta_tokens.shape[0]
    past_len = page_table.shape[1] * PAGE
    lam_init = 0.8 - 0.6 * math.exp(-0.3 * 0)

    w_in_p, wq_p, wkv_p = _prep_proj_weights(w_in[0], w_uq[0], w_ukv[0])
    pconsts = (g_attn, w_in_p, g_cq, g_ckv, wq_p, wkv_p, _seg_ones(LANES), _seg_ones(DIFF_HD))
    gains = (g_qn_mla[0], g_kn_mla[0], g_qn_diff[0], g_kn_diff[0])
    tab_meta = _tables(jnp.arange(n_meta), *gains)
    tab_prompt = _tables(n_meta + jnp.arange(seq), *gains)
    tab_new = jnp.tile(_tables(past_len + jnp.arange(n_new), *gains), (PAGE // n_new, 1))

    tm = min(256, seq)
    meta = _project(meta_tokens.astype(F32), tab_meta, 1, n_meta, pconsts)
    main = _project(x_prompt.reshape(n_b * seq, d), tab_prompt, n_b, tm, pconsts)
    new = _project(x_sample.reshape(n_seq * n_new, d), tab_new, (n_seq * n_new) // PAGE, PAGE, pconsts)

    lam = jnp.concatenate([lambda_q1, lambda_k1, lambda_q2, lambda_k2, jnp.full((1, DIFF_HD), lam_init, F32)], axis=0)
    g_o = jnp.tile(g_o_mla, (1, LANES // MLA_V))
    tq = min(512, seq)
    oa_p = _prompt_attn(main[4], main[5], main[6], meta[5], meta[6], g_o, lam, n_b, tq, True, ATTN_PAIRS, 1.0)
    ob_p = _prompt_attn(main[7], main[8], main[9], meta[8], meta[9], g_subln, lam, n_b, tq, False, ATTN_PAIRS,
                        1.0 - lam_init)

    oa_s, ob_s = _decode_attn(new, cache_ckv[0], cache_krope[0], cache_dk[0], cache_dv[0], page_table,
                              w_ukv[0], gains, g_o, g_subln, lam, lam_init, n_seq, n_new)

    wo = w_o[0].astype(BF16)
    n_a = MLA_HEADS * MLA_V
    wg = jnp.concatenate([w_gate_group[0].T, jnp.zeros((8 - N_GROUPS, d), F32),
                          w_gate_expert[0].reshape(d, N_GROUPS * EXP_PER_GROUP).T], axis=0)
    wg_hi = wg.astype(BF16)
    wg_lo = (wg - wg_hi.astype(F32)).astype(BF16)
    bias = jnp.concatenate([b_gate_group[0], jnp.zeros((8 - N_GROUPS,), F32), b_gate_expert[0].reshape(-1)])[:, None]
    gconsts = (wo[:n_a], wo[n_a:], g_ffn, wg_hi, wg_lo, bias)
    n_p, n_s = n_b * seq, n_seq * n_new
    n_all = n_p + n_s
    hp_all, gsel, comb = _out_gate((x_prompt.reshape(n_p, d), oa_p, ob_p), (x_sample.reshape(n_s, d), oa_s, ob_s),
                                   gconsts)
    y_all = _moe(hp_all, n_all, gsel[0], comb, g_ffn, w1[0], w3[0], w2[0], 256)
    y_p, y_s = y_all[:n_p], y_all[n_p:n_all]

    def with_meta(m, x, tail):
        m = jnp.broadcast_to(m[None], (n_b,) + m.shape)
        return jnp.concatenate([m, x.reshape((n_b, seq) + m.shape[2:])], axis=1).reshape((1, n_b, n_meta + seq) + tail)

    return (y_p.reshape(n_b, seq, d), y_s.reshape(n_seq, n_new, d),
            with_meta(meta[0], main[0], (KV_LORA,)), with_meta(meta[1], main[1], (MLA_ROPE,)),
            with_meta(meta[2], main[2], (DIFF_MAPS, DIFF_HD)), with_meta(meta[3], main[3], (DIFF_HEADS, DIFF_V)),
            new[0].reshape(1, n_seq, n_new, KV_LORA), new[1].reshape(1, n_seq, n_new, MLA_ROPE),
            new[2].reshape(1, n_seq, n_new, DIFF_MAPS, DIFF_HD), new[3].reshape(1, n_seq, n_new, DIFF_HEADS, DIFF_V))
```

```python
import functools
import math

import jax
import jax.numpy as jnp
from jax import lax
from jax.experimental import pallas as pl
from jax.experimental.pallas import tpu as pltpu

F32 = jnp.float32
BF16 = jnp.bfloat16
I32 = jnp.int32

RMS_EPS = 1e-6
ROPE_THETA = 10000.0
NEG_INF = -1e30
PAGE = 128
LANES = 128

MLA_HEADS = 8
MLA_NOPE = 64
MLA_ROPE = 32
MLA_QK = MLA_NOPE + MLA_ROPE
MLA_V = 64
Q_LORA = 256
KV_LORA = 128
DIFF_HEADS = 4
DIFF_HD = 64
DIFF_V = 128
DIFF_MAPS = 2 * DIFF_HEADS
N_GROUPS = 4
EXP_PER_GROUP = 8
D_EXPERT = 256

VMEM_LIMIT = 56 * 1024 * 1024

ATTN_PAIRS = 2
N_TAB = 12


def _dot(a, b):
    return jnp.dot(a, b, preferred_element_type=F32)


def _dot_nt(a, b):
    return lax.dot_general(a, b, (((1,), (1,)), ((), ())), preferred_element_type=F32)


def _rope_cs(pos, d):
    inv = 1.0 / (ROPE_THETA ** (jnp.arange(0, d, 2, dtype=F32) / d))
    ang = pos.astype(F32)[:, None] * inv[None, :]
    return jnp.cos(ang), jnp.sin(ang)


def _mla_tables(pos, g, scale):
    c, s = _rope_cs(pos, MLA_ROPE)
    n = pos.shape[0]
    h = MLA_ROPE // 2
    gn, g1, g2 = g[:MLA_NOPE], g[MLA_NOPE:MLA_NOPE + h], g[MLA_NOPE + h:]
    z = lambda w: jnp.zeros((n, w), F32)
    big_c = jnp.concatenate([jnp.broadcast_to(gn[None], (n, MLA_NOPE)), c * g1, c * g2, z(LANES - MLA_QK)], 1)
    big_a = jnp.concatenate([z(MLA_NOPE), -s * g2, z(LANES - MLA_NOPE - h)], 1)
    big_b = jnp.concatenate([z(MLA_NOPE + h), s * g1, z(LANES - MLA_QK)], 1)
    return [big_c * scale, big_a * scale, big_b * scale]


def _diff_tables(pos, g, scale):
    c, s = _rope_cs(pos, DIFF_HD)
    n = pos.shape[0]
    h = DIFF_HD // 2
    z = jnp.zeros((n, h), F32)
    c64 = jnp.concatenate([c * g[:h], c * g[h:]], 1)
    a64 = jnp.concatenate([-s * g[h:], z], 1)
    b64 = jnp.concatenate([z, s * g[:h]], 1)
    return [jnp.tile(t, (1, 2)) * scale for t in (c64, a64, b64)]


def _tables(pos, g_qn_mla, g_kn_mla, g_qn_diff, g_kn_diff):
    log2e = math.log2(math.e)
    t = (_mla_tables(pos, g_qn_mla, MLA_QK ** -0.5 * log2e) + _mla_tables(pos, g_kn_mla, 1.0)
         + _diff_tables(pos, g_qn_diff, DIFF_HD ** -0.5 * log2e) + _diff_tables(pos, g_kn_diff, 1.0))
    return jnp.concatenate(t, axis=1)


def _prep_proj_weights(w_in, w_uq, w_ukv):
    d = w_in.shape[0]
    off_ckv, off_kr = Q_LORA, Q_LORA + KV_LORA
    off_rest = off_kr + MLA_ROPE
    kr = w_in[:, off_kr:off_rest]
    z = jnp.zeros((d, MLA_ROPE), w_in.dtype)
    w_in_p = jnp.concatenate([w_in[:, :off_kr], kr, z, kr, z, w_in[:, off_rest:]], axis=1).astype(BF16)
    wq = w_uq.reshape(Q_LORA, MLA_HEADS, MLA_QK)
    wq = jnp.pad(wq, ((0, 0), (0, 0), (0, LANES - MLA_QK))).reshape(Q_LORA, MLA_HEADS * LANES).astype(BF16)
    wkv = w_ukv.reshape(KV_LORA, MLA_HEADS, MLA_NOPE + MLA_V)
    wk = jnp.pad(wkv[..., :MLA_NOPE], ((0, 0), (0, 0), (0, LANES - MLA_NOPE))).reshape(KV_LORA, MLA_HEADS * LANES)
    wv = wkv[..., MLA_NOPE:].reshape(KV_LORA, MLA_HEADS * MLA_V)
    wkv_p = jnp.concatenate([wk, wv], axis=1).astype(BF16)
    return w_in_p, wq, wkv_p


def _seg_ones(width):
    i = jnp.arange(LANES)
    return (i[:, None] // width == i[None, :] // width).astype(BF16)


def _segsum(sq, e):
    hi = sq.astype(BF16)
    lo = (sq - hi.astype(F32)).astype(BF16)
    return _dot(hi, e) + _dot(lo, e)


def _norm_rope(x, tab_ref, t0, e, inv_n, sh_up, sh_dn):
    r = lax.rsqrt(_segsum(x * x, e) * inv_n + RMS_EPS)
    c = tab_ref[:, (t0 + 0) * LANES:(t0 + 1) * LANES]
    a = tab_ref[:, (t0 + 1) * LANES:(t0 + 2) * LANES]
    b = tab_ref[:, (t0 + 2) * LANES:(t0 + 3) * LANES]
    rot = x * c + pltpu.roll(x, sh_up, 1) * a + pltpu.roll(x, sh_dn, 1) * b
    return rot * r


def _proj_kernel(x_ref, tab_ref, gattn_ref, win_ref, gcq_ref, gckv_ref, wuq_ref, wkv_ref, e128_ref, e64_ref,
                 ckv_ref, kr_ref, dk32_ref, dv32_ref, qm_ref, km_ref, vm_ref, dq_ref, dk16_ref, dv16_ref):
    x = x_ref[...]
    u = x * lax.rsqrt(jnp.mean(x * x, axis=-1, keepdims=True) + RMS_EPS) * gattn_ref[...]
    proj = _dot(u.astype(BF16), win_ref[...])
    o_ckv, o_kr, o_dq = Q_LORA, Q_LORA + KV_LORA, Q_LORA + KV_LORA + LANES
    o_dk = o_dq + DIFF_MAPS * DIFF_HD
    o_dv = o_dk + DIFF_MAPS * DIFF_HD

    cq = proj[:, :Q_LORA]
    cqn = cq * lax.rsqrt(jnp.mean(cq * cq, axis=-1, keepdims=True) + RMS_EPS) * gcq_ref[...]
    ckv = proj[:, o_ckv:o_kr]
    ckvn = ckv * lax.rsqrt(jnp.mean(ckv * ckv, axis=-1, keepdims=True) + RMS_EPS) * gckv_ref[...]
    ckv_ref[...] = ckvn
    krs = proj[:, o_kr:o_dq]
    kr_ref[...] = krs[:, :MLA_ROPE]

    e128 = e128_ref[...]
    e64 = e64_ref[...]
    up, dn = LANES - MLA_ROPE // 2, MLA_ROPE // 2
    qm = _dot(cqn.astype(BF16), wuq_ref[...])
    for h in range(MLA_HEADS):
        sl = slice(h * LANES, (h + 1) * LANES)
        qm_ref[:, sl] = _norm_rope(qm[:, sl], tab_ref, 0, e128, 1.0 / MLA_QK, up, dn).astype(BF16)

    kv = _dot(ckvn.astype(BF16), wkv_ref[...])
    lane = lax.broadcasted_iota(I32, (1, LANES), 1)
    kr_mid = krs * ((lane >= MLA_NOPE) & (lane < MLA_QK)).astype(F32)
    for h in range(MLA_HEADS):
        sl = slice(h * LANES, (h + 1) * LANES)
        km_ref[:, sl] = _norm_rope(kv[:, sl] + kr_mid, tab_ref, 3, e128, 1.0 / MLA_QK, up, dn).astype(BF16)
    vm_ref[...] = kv[:, MLA_HEADS * LANES:].astype(BF16)

    up, dn = LANES - DIFF_HD // 2, DIFF_HD // 2
    for j in range(DIFF_MAPS * DIFF_HD // LANES):
        sl = slice(j * LANES, (j + 1) * LANES)
        dq_ref[:, sl] = _norm_rope(proj[:, o_dq + j * LANES:o_dq + (j + 1) * LANES], tab_ref, 6, e64,
                                   1.0 / DIFF_HD, up, dn).astype(BF16)
        dk = _norm_rope(proj[:, o_dk + j * LANES:o_dk + (j + 1) * LANES], tab_ref, 9, e64, 1.0 / DIFF_HD, up, dn)
        dk32_ref[:, sl] = dk
        dk16_ref[:, sl] = dk.astype(BF16)
    dv = proj[:, o_dv:]
    dv32_ref[...] = dv
    dv16_ref[...] = dv.astype(BF16)


def _project(x, tab, n_batch, tm, consts):
    t, d = x.shape
    n_i = t // n_batch // tm
    row = lambda b, i: (b * n_i + i, 0)
    fixed = lambda b, i: (0, 0)
    gattn, win, gcq, gckv, wuq, wkv, e128, e64 = consts
    widths = [(KV_LORA, F32), (MLA_ROPE, F32), (DIFF_MAPS * DIFF_HD, F32), (DIFF_HEADS * DIFF_V, F32),
              (MLA_HEADS * LANES, BF16), (MLA_HEADS * LANES, BF16), (MLA_HEADS * MLA_V, BF16),
              (DIFF_MAPS * DIFF_HD, BF16), (DIFF_MAPS * DIFF_HD, BF16), (DIFF_HEADS * DIFF_V, BF16)]
    return pl.pallas_call(
        _proj_kernel,
        grid=(n_batch, n_i),
        in_specs=[pl.BlockSpec((tm, d), row), pl.BlockSpec((tm, N_TAB * LANES), lambda b, i: (i, 0))]
        + [pl.BlockSpec(c.shape, fixed) for c in consts],
        out_specs=[pl.BlockSpec((tm, w), row) for w, _ in widths],
        out_shape=[jax.ShapeDtypeStruct((t, w), dt) for w, dt in widths],
        compiler_params=pltpu.CompilerParams(dimension_semantics=("arbitrary", "arbitrary"),
                                             vmem_limit_bytes=VMEM_LIMIT),
        name="project",
    )(x, tab, *consts)


def _lane_tile(x, width):
    if width <= LANES:
        return x[:, :width]
    return jnp.concatenate([x] * (width // LANES), axis=1)


def _flash_update(a, q, k, v, m_scr, l_scr, acc_scr):
    s = _dot_nt(q, k)
    m_prev = m_scr[a]
    m_next = jnp.maximum(m_prev, jnp.max(s, axis=1, keepdims=True))
    alpha = jnp.exp2(m_prev - m_next)
    p = jnp.exp2(s - _lane_tile(m_next, s.shape[1]))
    l_scr[a] = alpha * l_scr[a] + jnp.sum(p, axis=1, keepdims=True)
    acc_scr[a] = acc_scr[a] * alpha + _dot(p.astype(BF16), v)
    m_scr[a] = m_next


def _pattn_kernel(q_ref, k_ref, v_ref, kmeta_ref, vmeta_ref, gain_ref, lam_ref, o_ref, m_scr, l_scr, acc_scr,
                  *, mla, tq, n_pair, n_meta, post_scale):
    i = pl.program_id(2)
    lane = lax.broadcasted_iota(I32, (1, LANES), 1)
    lo_half = lane < LANES // 2
    slab = lambda j: slice(j * LANES, (j + 1) * LANES)
    maps = range(2 * n_pair)
    if mla:
        qs = [q_ref[:, slab(a)] for a in maps]
        ksl = [slab(a) for a in maps]
    else:
        half = [lo_half.astype(BF16), (1 - lo_half.astype(I32)).astype(BF16)]
        qs = [q_ref[:, slab(a // 2)] * half[a % 2] for a in maps]
        ksl = [slab(a // 2) for a in maps]

    k0 = pl.multiple_of(i * tq, tq)
    causal = lax.broadcasted_iota(I32, (tq, tq), 0) >= lax.broadcasted_iota(I32, (tq, tq), 1)
    meta_ok = lane < n_meta
    for a in maps:
        s1 = jnp.where(causal, _dot_nt(qs[a], k_ref[pl.ds(k0, tq), ksl[a]]), NEG_INF)
        s2 = jnp.where(meta_ok, _dot_nt(qs[a], kmeta_ref[:, ksl[a]]), NEG_INF)
        m = jnp.maximum(jnp.max(s1, axis=1, keepdims=True), jnp.max(s2, axis=1, keepdims=True))
        p1 = jnp.exp2(s1 - m)
        p2 = jnp.exp2(s2 - m)
        m_scr[a] = jnp.broadcast_to(m, (tq, LANES))
        l_scr[a] = jnp.broadcast_to(jnp.sum(p1, axis=1, keepdims=True) + jnp.sum(p2, axis=1, keepdims=True),
                                    (tq, LANES))
        acc_scr[a] = (_dot(p1.astype(BF16), v_ref[pl.ds(k0, tq), slab(a // 2)])
                      + _dot(p2.astype(BF16), vmeta_ref[:, slab(a // 2)]))

    def body(j, carry):
        kj = pl.multiple_of(j * tq, tq)
        for a in maps:
            _flash_update(a, qs[a], k_ref[pl.ds(kj, tq), ksl[a]], v_ref[pl.ds(kj, tq), slab(a // 2)],
                          m_scr, l_scr, acc_scr)
        return carry

    lax.fori_loop(0, i, body, 0)

    for pr in range(n_pair):
        o0 = acc_scr[2 * pr] / l_scr[2 * pr]
        o1 = acc_scr[2 * pr + 1] / l_scr[2 * pr + 1]
        if mla:
            o = jnp.where(lo_half, o0, o1)
            sq = o * o
            ss_lo = jnp.sum(jnp.where(lo_half, sq, 0.0), axis=1, keepdims=True)
            ss_hi = jnp.sum(jnp.where(lo_half, 0.0, sq), axis=1, keepdims=True)
            r = jnp.where(lo_half, lax.rsqrt(ss_lo * (2.0 / LANES) + RMS_EPS),
                          lax.rsqrt(ss_hi * (2.0 / LANES) + RMS_EPS))
        else:
            o = o0 - _lambda(lam_ref) * o1
            r = lax.rsqrt(jnp.mean(o * o, axis=1, keepdims=True) + RMS_EPS)
        o_ref[:, slab(pr)] = (o * r * gain_ref[...] * post_scale).astype(o_ref.dtype)


def _lambda(lam_ref):
    lam = lam_ref[...]
    a = jnp.exp(jnp.sum(lam[0:1] * lam[1:2], axis=1, keepdims=True))
    b = jnp.exp(jnp.sum(lam[2:3] * lam[3:4], axis=1, keepdims=True))
    return a - b + lam[4:5, 0:1]


def _prompt_attn(q, k, v, kmeta, vmeta, gain, lam, n_batch, tq, mla, n_pair, post_scale):
    t = q.shape[0]
    s = t // n_batch
    n_q = s // tq
    n_meta = kmeta.shape[0]
    assert n_meta <= LANES
    kmeta = jnp.pad(kmeta, ((0, LANES - n_meta), (0, 0)))
    vmeta = jnp.pad(vmeta, ((0, LANES - n_meta), (0, 0)))
    kw = (2 * LANES if mla else LANES) * n_pair
    vw = LANES * n_pair
    n_hp = q.shape[1] // kw
    kern = functools.partial(_pattn_kernel, mla=mla, tq=tq, n_pair=n_pair, n_meta=n_meta, post_scale=post_scale)
    return pl.pallas_call(
        kern,
        grid=(n_batch, n_hp, n_q),
        in_specs=[pl.BlockSpec((tq, kw), lambda b, h, i: (b * n_q + i, h)),
                  pl.BlockSpec((s, kw), lambda b, h, i: (b, h)),
                  pl.BlockSpec((s, vw), lambda b, h, i: (b, h)),
                  pl.BlockSpec((LANES, kw), lambda b, h, i: (0, h)),
                  pl.BlockSpec((LANES, vw), lambda b, h, i: (0, h)),
                  pl.BlockSpec((1, LANES), lambda b, h, i: (0, 0)),
                  pl.BlockSpec(lam.shape, lambda b, h, i: (0, 0))],
        out_specs=pl.BlockSpec((tq, vw), lambda b, h, i: (b * n_q + i, h)),
        out_shape=jax.ShapeDtypeStruct((t, n_hp * vw), BF16),
        scratch_shapes=[pltpu.VMEM((2 * n_pair, tq, LANES), F32)] * 3,
        compiler_params=pltpu.CompilerParams(dimension_semantics=("arbitrary",) * 3, vmem_limit_bytes=VMEM_LIMIT),
        name="prompt_attn_mla" if mla else "prompt_attn_diff",
    )(q, k, v, kmeta, vmeta, gain, lam)


Q_ABS_W = 2 * LANES
PAGES_PER_STEP = 32
DMA_QUEUE_OF_POOL = (1, 0, 0, 1)


def _absorb_weights(w_ukv, g_kn_mla):
    wk = w_ukv.reshape(KV_LORA, MLA_HEADS, MLA_NOPE + MLA_V)[..., :MLA_NOPE]
    h = MLA_ROPE // 2
    gn, g1, g2 = g_kn_mla[:MLA_NOPE], g_kn_mla[MLA_NOPE:MLA_NOPE + h], g_kn_mla[MLA_NOPE + h:]
    m = jnp.zeros((MLA_HEADS, LANES, Q_ABS_W), F32)
    m = m.at[:, :MLA_NOPE, :KV_LORA].set(jnp.transpose(wk, (1, 2, 0)) * gn[None, :, None])
    i = jnp.arange(h)
    m = m.at[:, MLA_NOPE + i, LANES + i].set(g1)
    m = m.at[:, MLA_NOPE + h + i, LANES + h + i].set(g2)
    m = m.at[:, MLA_NOPE + h + i, LANES + MLA_ROPE + i].set(g1)
    m = m.at[:, MLA_NOPE + i, LANES + MLA_ROPE + h + i].set(-g2)
    return m.astype(BF16)


def _absorb_kernel(q_ref, m_ref, o_ref):
    n = q_ref.shape[0] // 8
    for h in range(MLA_HEADS):
        o_ref[:, h, :, :] = _dot(q_ref[:, h * LANES:(h + 1) * LANES], m_ref[h]).reshape(n, 8, Q_ABS_W)


def _absorb(qm, m_abs, n_new):
    t = qm.shape[0]
    tm = PAGE
    assert n_new == 8 and t % tm == 0
    return pl.pallas_call(
        _absorb_kernel,
        grid=(t // tm,),
        in_specs=[pl.BlockSpec((tm, MLA_HEADS * LANES), lambda i: (i, 0)),
                  pl.BlockSpec(m_abs.shape, lambda i: (0, 0, 0))],
        out_specs=pl.BlockSpec((tm // n_new, MLA_HEADS, n_new, Q_ABS_W), lambda i: (i, 0, 0, 0)),
        out_shape=jax.ShapeDtypeStruct((t // n_new, MLA_HEADS, n_new, Q_ABS_W), F32),
        compiler_params=pltpu.CompilerParams(dimension_semantics=("arbitrary",), vmem_limit_bytes=VMEM_LIMIT),
        name="absorb_q",
    )(qm, m_abs)


def _softmax_step(s, m_scr, l_scr, acc_scr):
    m_prev = m_scr[...]
    m_next = jnp.maximum(m_prev, jnp.max(s, axis=1, keepdims=True))
    alpha = jnp.exp2(m_prev - m_next)
    w = s.shape[1]
    p = jnp.exp2(s - _lane_tile(m_next, w))
    l_scr[...] = alpha * l_scr[...] + jnp.sum(p, axis=1, keepdims=True)
    acc_scr[...] = acc_scr[...] * alpha
    m_scr[...] = m_next
    return p


def _decode_kernel(pt_ref, qabs_ref, qrow_ref, knew_ref, vnew_ref, qd_ref, kdnew_ref, vdnew_ref, tabc_ref, tabs_ref,
                   wukt_ref, wuv_ref, go_ref, gsub_ref, lam_ref, ckv_hbm, kr_hbm, dk_hbm, dv_hbm, oa_ref, ob_ref,
                   ma, la, acca, md, ld, accd, sa, sd, ckv_buf, kr_buf, dk_buf, dv_buf, sems, *, n_pg, post_scale):
    c = pl.program_id(1)
    n_chunks = pl.num_programs(1)
    step = pl.program_id(0) * n_chunks + c
    last_step = pl.num_programs(0) * n_chunks - 1
    slot = step % 2
    other = 1 - slot
    nxt = jnp.minimum(step + 1, last_step)
    n_rows = MLA_HEADS * 8
    pools = ((ckv_hbm, ckv_buf), (kr_hbm, kr_buf), (dk_hbm, dk_buf), (dv_hbm, dv_buf))

    def fetch_page(step_idx, buf_slot, p):
        page = pt_ref[step_idx * n_pg + p]
        for k, (pool, buf) in enumerate(pools):
            pltpu.make_async_copy(pool.at[page], buf.at[buf_slot, p], sems.at[k, buf_slot]).start(
                priority=DMA_QUEUE_OF_POOL[k])

    def wait_pages(buf_slot):
        for k, (pool, buf) in enumerate(pools):
            pltpu.make_async_copy(pool.at[pl.ds(0, n_pg)], buf.at[buf_slot], sems.at[k, buf_slot]).wait()

    @pl.when(step == 0)
    def _():
        for p in range(n_pg):
            fetch_page(0, 0, p)

    @pl.when(c == 0)
    def _():
        for m_scr, l_scr, acc_scr in ((ma, la, acca), (md, ld, accd)):
            m_scr[...] = jnp.full(m_scr.shape, NEG_INF, F32)
            l_scr[...] = jnp.zeros(l_scr.shape, F32)
            acc_scr[...] = jnp.zeros(acc_scr.shape, F32)

    wait_pages(slot)
    qabs = qabs_ref[...].reshape(n_rows, Q_ABS_W)
    lhs_lat = jnp.concatenate([qabs[:, :KV_LORA], wukt_ref[...]], axis=0)
    q_cs = qabs[:, LANES:LANES + 2 * MLA_ROPE]
    qd = qd_ref[...]
    ri = lax.broadcasted_iota(I32, (n_rows, DIFF_MAPS * DIFF_HD), 0)
    ci = lax.broadcasted_iota(I32, (n_rows, DIFF_MAPS * DIFF_HD), 1)
    q_bd = jnp.where(ri // 8 == ci // DIFF_HD, jnp.concatenate([qd] * DIFF_MAPS, axis=1), 0.0)

    for p in range(n_pg):
        cols = slice(p * PAGE, (p + 1) * PAGE)
        fetch_page(nxt, other, p)
        kr_t = kr_buf[slot, p]
        big = _dot_nt(lhs_lat, ckv_buf[slot, p])
        kn_t = big[n_rows:]
        ss = jnp.sum((kn_t * kn_t).reshape(MLA_HEADS, MLA_NOPE, PAGE), axis=1)
        ss = ss + jnp.sum(kr_t * kr_t, axis=0, keepdims=True)
        r_t = lax.rsqrt(ss * (1.0 / MLA_QK) + RMS_EPS)
        f_cs = jnp.concatenate([kr_t * tabc_ref[:, cols], kr_t * tabs_ref[:, cols]], axis=0)
        s_pg = big[:n_rows] + _dot(q_cs, f_cs)
        for h in range(MLA_HEADS):
            sa[h * 8:(h + 1) * 8, cols] = s_pg[h * 8:(h + 1) * 8, :] * r_t[h:h + 1, :]
        sd[:, cols] = _dot(q_bd, dk_buf[slot, p].reshape(DIFF_MAPS * DIFF_HD, PAGE))

    pa = _softmax_step(sa[...], ma, la, acca)
    pd = _softmax_step(sd[...], md, ld, accd)
    acc_a = acca[...]
    acc_d = accd[...]
    head_of_row = lax.broadcasted_iota(I32, (n_rows, PAGE), 0) // 16
    for p in range(n_pg):
        cols = slice(p * PAGE, (p + 1) * PAGE)
        acc_a = acc_a + _dot(pa[:, cols], ckv_buf[slot, p])
        v_stack = jnp.concatenate([dv_buf[slot, p, pl.ds(h, PAGE, stride=DIFF_HEADS), :]
                                   for h in range(DIFF_HEADS)], axis=0)
        pd_pg = pd[:, cols]
        p_wide = jnp.concatenate([jnp.where(head_of_row == h, pd_pg, 0.0) for h in range(DIFF_HEADS)], axis=1)
        acc_d = acc_d + _dot(p_wide, v_stack)
    acca[...] = acc_a
    accd[...] = acc_d

    @pl.when(step == last_step)
    def _():
        wait_pages(other)

    @pl.when(c == pl.num_programs(1) - 1)
    def _():
        ri = lax.broadcasted_iota(I32, (n_rows, n_rows), 0)
        ci = lax.broadcasted_iota(I32, (n_rows, n_rows), 1)
        valid = (ri // 8 == ci // 8) & (ci % 8 <= ri % 8)

        s_new = jnp.where(valid, _dot_nt(qrow_ref[...], knew_ref[...]), NEG_INF)
        p_new = _softmax_step(s_new, ma, la, acca)
        o_new = _dot(p_new, vnew_ref[...])
        lat = acca[...]
        lat_hi = lat.astype(BF16).astype(F32)
        lat_lo = lat - lat_hi
        l_a = la[...]
        for h in range(MLA_HEADS):
            rows = slice(h * 8, (h + 1) * 8)
            wv = wuv_ref[h]
            o = (_dot(lat_hi[rows], wv) + _dot(lat_lo[rows], wv) + o_new[rows]) / l_a[rows, :MLA_V]
            r = lax.rsqrt(jnp.mean(o * o, axis=1, keepdims=True) + RMS_EPS)
            oa_ref[rows, :] = (o * r * go_ref[...]).astype(oa_ref.dtype)

        sd_new = jnp.where(valid, _dot_nt(qd, kdnew_ref[...]), NEG_INF)
        pd_new = _softmax_step(sd_new, md, ld, accd)
        od = (accd[...] + _dot(pd_new, vdnew_ref[...])) / ld[...]
        lam = _lambda(lam_ref)
        for h in range(DIFF_HEADS):
            o = od[2 * h * 8:(2 * h + 1) * 8] - lam * od[(2 * h + 1) * 8:(2 * h + 2) * 8]
            r = lax.rsqrt(jnp.mean(o * o, axis=1, keepdims=True) + RMS_EPS)
            ob_ref[h * 8:(h + 1) * 8, :] = (o * r * gsub_ref[...] * post_scale).astype(ob_ref.dtype)


def _decode_attn(new, pool_ckv, pool_kr, pool_dk, pool_dv, page_table, w_ukv, gains, g_o, g_subln, lam, lam_init,
                 n_seq, n_new):
    assert n_new == 8
    n_pages = page_table.shape[1]
    n_pg = min(PAGES_PER_STEP, n_pages)
    assert n_pages % n_pg == 0
    past_len = n_pages * PAGE
    qm, km, vm, dq, dk16, dv16 = new[4:10]

    def rows_by_head(x, n_h):
        w = x.shape[1] // n_h
        x = jnp.transpose(x.reshape(n_seq, n_new, n_h, w), (0, 2, 1, 3))
        return x.reshape(n_seq, n_h * n_new, w).astype(F32)

    qabs = _absorb(qm, _absorb_weights(w_ukv, gains[1]), n_new)
    qrow = rows_by_head(qm, MLA_HEADS)
    knew = rows_by_head(km, MLA_HEADS)
    vnew = rows_by_head(vm, MLA_HEADS)
    qd = rows_by_head(dq, DIFF_MAPS)
    kdnew = rows_by_head(dk16, DIFF_MAPS)
    vdnew = jnp.repeat(rows_by_head(dv16, DIFF_HEADS).reshape(n_seq, DIFF_HEADS, n_new, DIFF_V), 2, axis=1)
    vdnew = vdnew.reshape(n_seq, DIFF_MAPS * n_new, DIFF_V)

    pool_kr = jnp.transpose(pool_kr, (0, 2, 1))
    pool_dk = jnp.transpose(pool_dk, (0, 2, 3, 1))
    pool_dv = pool_dv.reshape(pool_dv.shape[0], PAGE * DIFF_HEADS, DIFF_V)
    cs, sn = _rope_cs(jnp.arange(past_len), MLA_ROPE)
    tabc = jnp.concatenate([cs, cs], axis=1).T
    tabs = jnp.concatenate([sn, sn], axis=1).T
    wkv = w_ukv.reshape(KV_LORA, MLA_HEADS, MLA_NOPE + MLA_V)
    wukt = wkv[..., :MLA_NOPE].reshape(KV_LORA, MLA_HEADS * MLA_NOPE).T
    wuv = jnp.transpose(wkv[..., MLA_NOPE:], (1, 0, 2))

    per_seq = lambda a: pl.BlockSpec((None,) + a.shape[1:], lambda s, c, pt: (s,) + (0,) * (a.ndim - 1))
    fixed = lambda a: pl.BlockSpec(a.shape, lambda s, c, pt: (0,) * a.ndim)
    tab_spec = pl.BlockSpec((MLA_ROPE, n_pg * PAGE), lambda s, c, pt: (0, c))

    seq_in = (qabs, qrow, knew, vnew, qd, kdnew, vdnew)
    consts = (wukt, wuv, g_o[:, :MLA_V], g_subln, lam)
    pools = (pool_ckv, pool_kr, pool_dk, pool_dv)
    n_rows = MLA_HEADS * n_new
    grid_spec = pltpu.PrefetchScalarGridSpec(
        num_scalar_prefetch=1,
        grid=(n_seq, n_pages // n_pg),
        in_specs=[per_seq(a) for a in seq_in] + [tab_spec, tab_spec] + [fixed(a) for a in consts]
        + [pl.BlockSpec(memory_space=pl.ANY)] * len(pools),
        out_specs=[pl.BlockSpec((None, n_rows, MLA_V), lambda s, c, pt: (s, 0, 0)),
                   pl.BlockSpec((None, DIFF_HEADS * n_new, DIFF_V), lambda s, c, pt: (s, 0, 0))],
        scratch_shapes=[pltpu.VMEM((n_rows, LANES), F32)] * 6
        + [pltpu.VMEM((n_rows, n_pg * PAGE), F32)] * 2
        + [pltpu.VMEM((2, n_pg) + pool.shape[1:], F32) for pool in pools]
        + [pltpu.SemaphoreType.DMA((len(pools), 2))],
    )
    oa, ob = pl.pallas_call(
        functools.partial(_decode_kernel, n_pg=n_pg, post_scale=1.0 - lam_init),
        grid_spec=grid_spec,
        out_shape=[jax.ShapeDtypeStruct((n_seq, n_rows, MLA_V), BF16),
                   jax.ShapeDtypeStruct((n_seq, DIFF_HEADS * n_new, DIFF_V), BF16)],
        compiler_params=pltpu.CompilerParams(dimension_semantics=("arbitrary", "arbitrary"),
                                             vmem_limit_bytes=VMEM_LIMIT),
        name="decode_attn",
    )(page_table.reshape(-1), *seq_in, tabc, tabs, *consts, *pools)

    def tokens_major(o, n_h):
        w = o.shape[2]
        return jnp.transpose(o.reshape(n_seq, n_h, n_new, w), (0, 2, 1, 3)).reshape(n_seq * n_new, n_h * w)

    return tokens_major(oa, MLA_HEADS), tokens_major(ob, DIFF_HEADS)


def _split_bf16(x):
    hi = x.astype(BF16)
    return hi, (x - hi.astype(F32)).astype(BF16)


def _out_gate_kernel(x_ref, oa_ref, ob_ref, woa_ref, wob_ref, gffn_ref, wg_hi_ref, wg_lo_ref, bias_ref,
                     hp_ref, gsel_ref, comb_ref):
    hp = x_ref[...] + _dot(oa_ref[...], woa_ref[...]) + _dot(ob_ref[...], wob_ref[...])
    hp_ref[...] = hp
    h = hp * lax.rsqrt(jnp.mean(hp * hp, axis=-1, keepdims=True) + RMS_EPS) * gffn_ref[...]
    h_hi, h_lo = _split_bf16(h)
    w_hi = wg_hi_ref[...]
    lt = _dot_nt(w_hi, h_hi) + _dot_nt(w_hi, h_lo) + _dot_nt(wg_lo_ref[...], h_hi) + bias_ref[...]

    gl = [lt[g:g + 1, :] for g in range(N_GROUPS)]
    gmax = functools.reduce(jnp.maximum, gl)
    gsel = jnp.full(gmax.shape, N_GROUPS - 1, I32)
    for g in range(N_GROUPS - 2, -1, -1):
        gsel = jnp.where(gl[g] == gmax, g, gsel)
    gw = 1.0 / functools.reduce(lambda a, b: a + b, [jnp.exp(x - gmax) for x in gl])

    base = 8
    el = []
    for e in range(EXP_PER_GROUP):
        v = lt[base + (N_GROUPS - 1) * EXP_PER_GROUP + e:base + (N_GROUPS - 1) * EXP_PER_GROUP + e + 1, :]
        for g in range(N_GROUPS - 2, -1, -1):
            v = jnp.where(gsel == g, lt[base + g * EXP_PER_GROUP + e:base + g * EXP_PER_GROUP + e + 1, :], v)
        el.append(v)
    v1 = functools.reduce(jnp.maximum, el)
    i1 = jnp.full(v1.shape, EXP_PER_GROUP - 1, I32)
    for e in range(EXP_PER_GROUP - 2, -1, -1):
        i1 = jnp.where(el[e] == v1, e, i1)
    rest = [jnp.where(i1 == e, -jnp.inf, el[e]) for e in range(EXP_PER_GROUP)]
    v2 = functools.reduce(jnp.maximum, rest)
    i2 = jnp.full(v1.shape, EXP_PER_GROUP - 1, I32)
    for e in range(EXP_PER_GROUP - 2, -1, -1):
        i2 = jnp.where(rest[e] == v2, e, i2)
    e2 = jnp.exp(v2 - v1)
    w1 = gw / (1.0 + e2)
    w2 = gw * e2 / (1.0 + e2)
    gsel_ref[...] = gsel
    for e in range(EXP_PER_GROUP):
        comb_ref[e:e + 1, :] = jnp.where(i1 == e, w1, jnp.where(i2 == e, w2, 0.0))


def _out_gate_both_kernel(xp_ref, oap_ref, obp_ref, xs_ref, oas_ref, obs_ref, *rest, n_prompt_tiles):
    i = pl.program_id(0)

    @pl.when(i < n_prompt_tiles)
    def _():
        _out_gate_kernel(xp_ref, oap_ref, obp_ref, *rest)

    @pl.when(i >= n_prompt_tiles)
    def _():
        _out_gate_kernel(xs_ref, oas_ref, obs_ref, *rest)


def _out_gate(prompt, sample, consts):
    (xp, oap, obp), (xs, oas, obs) = prompt, sample
    (n_p, d), n_s = xp.shape, xs.shape[0]
    tm = math.gcd(256, n_s)
    assert n_p % tm == 0
    npt, n_t = n_p // tm, (n_p + n_s) // tm
    p_row = lambda i: (jnp.minimum(i, npt - 1), 0)
    s_row = lambda i: (jnp.maximum(i - npt, 0), 0)
    fixed = lambda i: (0, 0)
    specs = lambda arrs, idx: [pl.BlockSpec((tm, a.shape[1]), idx) for a in arrs]
    return pl.pallas_call(
        functools.partial(_out_gate_both_kernel, n_prompt_tiles=npt),
        grid=(n_t,),
        in_specs=specs(prompt, p_row) + specs(sample, s_row) + [pl.BlockSpec(c.shape, fixed) for c in consts],
        out_specs=[pl.BlockSpec((tm, d), lambda i: (i, 0)), pl.BlockSpec((1, tm), lambda i: (0, i)),
                   pl.BlockSpec((EXP_PER_GROUP, tm), lambda i: (0, i))],
        out_shape=[jax.ShapeDtypeStruct((n_p + n_s, d), F32), jax.ShapeDtypeStruct((1, n_p + n_s), I32),
                   jax.ShapeDtypeStruct((EXP_PER_GROUP, n_p + n_s), F32)],
        compiler_params=pltpu.CompilerParams(dimension_semantics=("arbitrary",), vmem_limit_bytes=VMEM_LIMIT),
        name="out_gate",
    )(xp, oap, obp, xs, oas, obs, *consts)


def _moe_kernel(tg_ref, src_ref, dst_ref, hp_ref, comb_ref, gffn_ref, w1_ref, w3_ref, w2_ref,
                y_ref, xbuf, obuf, sem_in, sem_out, *, tmoe):
    t = pl.program_id(0)
    n_t = pl.num_programs(0)
    slot = t % 2
    other = 1 - slot
    nxt = jnp.minimum(t + 1, n_t - 1)
    chunk = tmoe // EXP_PER_GROUP

    def gather(tile, buf_slot, r, queue=0):
        row = src_ref[tile * tmoe + r]
        pltpu.make_async_copy(hp_ref.at[pl.ds(row, 1)], xbuf.at[buf_slot, pl.ds(r, 1)], sem_in.at[buf_slot]).start(
            priority=queue)

    def scatter(tile_shifted, buf_slot, r, queue=0):
        row = dst_ref[tile_shifted * tmoe + r]
        pltpu.make_async_copy(obuf.at[buf_slot, pl.ds(r, 1)], y_ref.at[pl.ds(row, 1)], sem_out.at[buf_slot]).start(
            priority=queue)

    def wait_gathers(buf_slot):
        pltpu.make_async_copy(hp_ref.at[pl.ds(0, tmoe)], xbuf.at[buf_slot], sem_in.at[buf_slot]).wait()

    def wait_scatters(buf_slot):
        pltpu.make_async_copy(obuf.at[buf_slot], y_ref.at[pl.ds(0, tmoe)], sem_out.at[buf_slot]).wait()

    @pl.when(t == 0)
    def _():
        obuf[...] = jnp.zeros_like(obuf)

        def first(r, c):
            gather(0, 0, r)
            return c

        lax.fori_loop(0, tmoe, first, 0, unroll=8)

    wait_gathers(slot)
    x = xbuf[slot]
    h = (x * lax.rsqrt(jnp.mean(x * x, axis=-1, keepdims=True) + RMS_EPS) * gffn_ref[...]).astype(BF16)
    comb = comb_ref[...]
    obuf[slot] = x
    for e in range(EXP_PER_GROUP):
        for r in range(e * chunk, (e + 1) * chunk):
            gather(nxt, other, r, r % 2)
            scatter(t, other, r, (r + 1) % 2)
        a = _dot(h, w1_ref[e])
        b = _dot(h, w3_ref[e])
        hid = a * (1.0 / (1.0 + jnp.exp(-a))) * b * comb[:, e:e + 1]
        obuf[slot] += _dot(hid.astype(BF16), w2_ref[e])
    wait_scatters(other)

    @pl.when(t == n_t - 1)
    def _():
        def last(r, c):
            scatter(t + 1, slot, r)
            return c

        lax.fori_loop(0, tmoe, last, 0, unroll=8)
        wait_scatters(slot)
        wait_gathers(other)


def _moe(hp_all, n_all, gsel, comb, g_ffn, w1, w3, w2, tmoe):
    d = hp_all.shape[1]
    n_tiles = n_all // tmoe + N_GROUPS
    onehot = (gsel[:, None] == jnp.arange(N_GROUPS, dtype=I32)[None, :]).astype(I32)
    csum = jnp.cumsum(onehot, axis=0)
    rank = jnp.take_along_axis(csum, gsel[:, None], axis=1)[:, 0] - 1
    g_tiles = (csum[-1] + tmoe - 1) // tmoe
    g_first = jnp.cumsum(g_tiles) - g_tiles
    dest = g_first[gsel] * tmoe + rank
    row_tok = jnp.full((n_tiles * tmoe,), -1, I32).at[dest].set(jnp.arange(n_all, dtype=I32))
    tile_group = jnp.sum(jnp.arange(n_tiles, dtype=I32)[:, None] >= g_first[None, 1:], axis=1).astype(I32)
    comb_sorted = jnp.where(row_tok[:, None] >= 0, comb.T[jnp.maximum(row_tok, 0)], 0.0)
    spare = n_all + jnp.arange(tmoe, dtype=I32)
    src_row = jnp.maximum(row_tok, 0)
    dst_row = jnp.where(row_tok >= 0, row_tok, jnp.tile(spare, n_tiles))
    dst_row = jnp.concatenate([spare, dst_row])

    w1g, w3g, w2g = w1.astype(BF16), w3.astype(BF16), w2.astype(BF16)
    group_w = lambda w: pl.BlockSpec((None,) + w.shape[1:], lambda t, tg, src, dst: (tg[t], 0, 0, 0))

    grid_spec = pltpu.PrefetchScalarGridSpec(
        num_scalar_prefetch=3,
        grid=(n_tiles,),
        in_specs=[pl.BlockSpec(memory_space=pl.ANY),
                  pl.BlockSpec((tmoe, EXP_PER_GROUP), lambda t, tg, src, dst: (t, 0)),
                  pl.BlockSpec((1, d), lambda t, tg, src, dst: (0, 0)),
                  group_w(w1g), group_w(w3g), group_w(w2g)],
        out_specs=pl.BlockSpec(memory_space=pl.ANY),
        scratch_shapes=[pltpu.VMEM((2, tmoe, d), F32), pltpu.VMEM((2, tmoe, d), F32),
                        pltpu.SemaphoreType.DMA((2,)), pltpu.SemaphoreType.DMA((2,))],
    )
    return pl.pallas_call(
        functools.partial(_moe_kernel, tmoe=tmoe),
        grid_spec=grid_spec,
        out_shape=jax.ShapeDtypeStruct((n_all + tmoe, d), F32),
        compiler_params=pltpu.CompilerParams(dimension_semantics=("arbitrary",), vmem_limit_bytes=VMEM_LIMIT),
        name="moe",
    )(tile_group, src_row, dst_row, hp_all, comb_sorted, g_ffn, w1g, w3g, w2g)


def kernel(x_prompt, x_sample, cache_ckv, cache_krope, cache_dk, cache_dv, page_table, meta_tokens, g_attn, w_in, g_cq, g_ckv, w_uq, g_qn_mla, w_ukv, g_kn_mla, g_o_mla, g_qn_diff, g_kn_diff, lambda_q1, lambda_k1, lambda_q2, lambda_k2, g_subln, w_o, g_ffn, w_gate_group, b_gate_group, w_gate_expert, b_gate_expert, w1, w3, w2):
    assert g_attn.shape[0] == 1, "single layer step"
    n_b, seq, d = x_prompt.shape
    n_seq, n_new, _ = x_sample.shape
    n_meta = meta_tokens.shape[0]
    past_len = page_table.shape[1] * PAGE
    lam_init = 0.8 - 0.6 * math.exp(-0.3 * 0)

    w_in_p, wq_p, wkv_p = _prep_proj_weights(w_in[0], w_uq[0], w_ukv[0])
    pconsts = (g_attn, w_in_p, g_cq, g_ckv, wq_p, wkv_p, _seg_ones(LANES), _seg_ones(DIFF_HD))
    gains = (g_qn_mla[0], g_kn_mla[0], g_qn_diff[0], g_kn_diff[0])
    tab_meta = _tables(jnp.arange(n_meta), *gains)
    tab_prompt = _tables(n_meta + jnp.arange(seq), *gains)
    tab_new = jnp.tile(_tables(past_len + jnp.arange(n_new), *gains), (PAGE // n_new, 1))

    tm = min(256, seq)
    meta = _project(meta_tokens.astype(F32), tab_meta, 1, n_meta, pconsts)
    main = _project(x_prompt.reshape(n_b * seq, d), tab_prompt, n_b, tm, pconsts)
    new = _project(x_sample.reshape(n_seq * n_new, d), tab_new, (n_seq * n_new) // PAGE, PAGE, pconsts)

    lam = jnp.concatenate([lambda_q1, lambda_k1, lambda_q2, lambda_k2, jnp.full((1, DIFF_HD), lam_init, F32)], axis=0)
    g_o = jnp.tile(g_o_mla, (1, LANES // MLA_V))
    tq = min(512, seq)
    oa_p = _prompt_attn(main[4], main[5], main[6], meta[5], meta[6], g_o, lam, n_b, tq, True, ATTN_PAIRS, 1.0)
    ob_p = _prompt_attn(main[7], main[8], main[9], meta[8], meta[9], g_subln, lam, n_b, tq, False, ATTN_PAIRS,
                        1.0 - lam_init)

    oa_s, ob_s = _decode_attn(new, cache_ckv[0], cache_krope[0], cache_dk[0], cache_dv[0], page_table,
                              w_ukv[0], gains, g_o, g_subln, lam, lam_init, n_seq, n_new)

    wo = w_o[0].astype(BF16)
    n_a = MLA_HEADS * MLA_V
    wg = jnp.concatenate([w_gate_group[0].T, jnp.zeros((8 - N_GROUPS, d), F32),
                          w_gate_expert[0].reshape(d, N_GROUPS * EXP_PER_GROUP).T], axis=0)
    wg_hi = wg.astype(BF16)
    wg_lo = (wg - wg_hi.astype(F32)).astype(BF16)
    bias = jnp.concatenate([b_gate_group[0], jnp.zeros((8 - N_GROUPS,), F32), b_gate_expert[0].reshape(-1)])[:, None]
    gconsts = (wo[:n_a], wo[n_a:], g_ffn, wg_hi, wg_lo, bias)
    n_p, n_s = n_b * seq, n_seq * n_new
    n_all = n_p + n_s
    hp_all, gsel, comb = _out_gate((x_prompt.reshape(n_p, d), oa_p, ob_p), (x_sample.reshape(n_s, d), oa_s, ob_s),
                                   gconsts)
    y_all = _moe(hp_all, n_all, gsel[0], comb, g_ffn, w1[0], w3[0], w2[0], 256)
    y_p, y_s = y_all[:n_p], y_all[n_p:n_all]

    def with_meta(m, x, tail):
        m = jnp.broadcast_to(m[None], (n_b,) + m.shape)
        return jnp.concatenate([m, x.reshape((n_b, seq) + m.shape[2:])], axis=1).reshape((1, n_b, n_meta + seq) + tail)

    return (y_p.reshape(n_b, seq, d), y_s.reshape(n_seq, n_new, d),
            with_meta(meta[0], main[0], (KV_LORA,)), with_meta(meta[1], main[1], (MLA_ROPE,)),
            with_meta(meta[2], main[2], (DIFF_MAPS, DIFF_HD)), with_meta(meta[3], main[3], (DIFF_HEADS, DIFF_V)),
            new[0].reshape(1, n_seq, n_new, KV_LORA), new[1].reshape(1, n_seq, n_new, MLA_ROPE),
            new[2].reshape(1, n_seq, n_new, DIFF_MAPS, DIFF_HD), new[3].reshape(1, n_seq, n_new, DIFF_HEADS, DIFF_V))
```

```python
import functools
import math

import jax
import jax.numpy as jnp
from jax import lax
from jax.experimental import pallas as pl
from jax.experimental.pallas import tpu as pltpu

F32 = jnp.float32
BF16 = jnp.bfloat16
I32 = jnp.int32

RMS_EPS = 1e-6
ROPE_THETA = 10000.0
NEG_INF = -1e30
PAGE = 128
LANES = 128

MLA_HEADS = 8
MLA_NOPE = 64
MLA_ROPE = 32
MLA_QK = MLA_NOPE + MLA_ROPE
MLA_V = 64
Q_LORA = 256
KV_LORA = 128
DIFF_HEADS = 4
DIFF_HD = 64
DIFF_V = 128
DIFF_MAPS = 2 * DIFF_HEADS
N_GROUPS = 4
EXP_PER_GROUP = 8
D_EXPERT = 256

VMEM_LIMIT = 56 * 1024 * 1024

ATTN_PAIRS = 2
N_TAB = 12


def _dot(a, b):
    return jnp.dot(a, b, preferred_element_type=F32)


def _dot_nt(a, b):
    return lax.dot_general(a, b, (((1,), (1,)), ((), ())), preferred_element_type=F32)


def _rope_cs(pos, d):
    inv = 1.0 / (ROPE_THETA ** (jnp.arange(0, d, 2, dtype=F32) / d))
    ang = pos.astype(F32)[:, None] * inv[None, :]
    return jnp.cos(ang), jnp.sin(ang)


def _mla_tables(pos, g, scale):
    c, s = _rope_cs(pos, MLA_ROPE)
    n = pos.shape[0]
    h = MLA_ROPE // 2
    gn, g1, g2 = g[:MLA_NOPE], g[MLA_NOPE:MLA_NOPE + h], g[MLA_NOPE + h:]
    z = lambda w: jnp.zeros((n, w), F32)
    big_c = jnp.concatenate([jnp.broadcast_to(gn[None], (n, MLA_NOPE)), c * g1, c * g2, z(LANES - MLA_QK)], 1)
    big_a = jnp.concatenate([z(MLA_NOPE), -s * g2, z(LANES - MLA_NOPE - h)], 1)
    big_b = jnp.concatenate([z(MLA_NOPE + h), s * g1, z(LANES - MLA_QK)], 1)
    return [big_c * scale, big_a * scale, big_b * scale]


def _diff_tables(pos, g, scale):
    c, s = _rope_cs(pos, DIFF_HD)
    n = pos.shape[0]
    h = DIFF_HD // 2
    z = jnp.zeros((n, h), F32)
    c64 = jnp.concatenate([c * g[:h], c * g[h:]], 1)
    a64 = jnp.concatenate([-s * g[h:], z], 1)
    b64 = jnp.concatenate([z, s * g[:h]], 1)
    return [jnp.tile(t, (1, 2)) * scale for t in (c64, a64, b64)]


def _tables(pos, g_qn_mla, g_kn_mla, g_qn_diff, g_kn_diff):
    log2e = math.log2(math.e)
    t = (_mla_tables(pos, g_qn_mla, MLA_QK ** -0.5 * log2e) + _mla_tables(pos, g_kn_mla, 1.0)
         + _diff_tables(pos, g_qn_diff, DIFF_HD ** -0.5 * log2e) + _diff_tables(pos, g_kn_diff, 1.0))
    return jnp.concatenate(t, axis=1)


def _prep_proj_weights(w_in, w_uq, w_ukv):
    d = w_in.shape[0]
    off_ckv, off_kr = Q_LORA, Q_LORA + KV_LORA
    off_rest = off_kr + MLA_ROPE
    kr = w_in[:, off_kr:off_rest]
    z = jnp.zeros((d, MLA_ROPE), w_in.dtype)
    w_in_p = jnp.concatenate([w_in[:, :off_kr], kr, z, kr, z, w_in[:, off_rest:]], axis=1).astype(BF16)
    wq = w_uq.reshape(Q_LORA, MLA_HEADS, MLA_QK)
    wq = jnp.pad(wq, ((0, 0), (0, 0), (0, LANES - MLA_QK))).reshape(Q_LORA, MLA_HEADS * LANES).astype(BF16)
    wkv = w_ukv.reshape(KV_LORA, MLA_HEADS, MLA_NOPE + MLA_V)
    wk = jnp.pad(wkv[..., :MLA_NOPE], ((0, 0), (0, 0), (0, LANES - MLA_NOPE))).reshape(KV_LORA, MLA_HEADS * LANES)
    wv = wkv[..., MLA_NOPE:].reshape(KV_LORA, MLA_HEADS * MLA_V)
    wkv_p = jnp.concatenate([wk, wv], axis=1).astype(BF16)
    return w_in_p, wq, wkv_p


def _seg_ones(width):
    i = jnp.arange(LANES)
    return (i[:, None] // width == i[None, :] // width).astype(BF16)


def _segsum(sq, e):
    hi = sq.astype(BF16)
    lo = (sq - hi.astype(F32)).astype(BF16)
    return _dot(hi, e) + _dot(lo, e)


def _norm_rope(x, tab_ref, t0, e, inv_n, sh_up, sh_dn):
    r = lax.rsqrt(_segsum(x * x, e) * inv_n + RMS_EPS)
    c = tab_ref[:, (t0 + 0) * LANES:(t0 + 1) * LANES]
    a = tab_ref[:, (t0 + 1) * LANES:(t0 + 2) * LANES]
    b = tab_ref[:, (t0 + 2) * LANES:(t0 + 3) * LANES]
    rot = x * c + pltpu.roll(x, sh_up, 1) * a + pltpu.roll(x, sh_dn, 1) * b
    return rot * r


def _proj_kernel(x_ref, tab_ref, gattn_ref, win_ref, gcq_ref, gckv_ref, wuq_ref, wkv_ref, e128_ref, e64_ref,
                 ckv_ref, kr_ref, dk32_ref, dv32_ref, qm_ref, km_ref, vm_ref, dq_ref, dk16_ref, dv16_ref):
    x = x_ref[...]
    u = x * lax.rsqrt(jnp.mean(x * x, axis=-1, keepdims=True) + RMS_EPS) * gattn_ref[...]
    proj = _dot(u.astype(BF16), win_ref[...])
    o_ckv, o_kr, o_dq = Q_LORA, Q_LORA + KV_LORA, Q_LORA + KV_LORA + LANES
    o_dk = o_dq + DIFF_MAPS * DIFF_HD
    o_dv = o_dk + DIFF_MAPS * DIFF_HD

    cq = proj[:, :Q_LORA]
    cqn = cq * lax.rsqrt(jnp.mean(cq * cq, axis=-1, keepdims=True) + RMS_EPS) * gcq_ref[...]
    ckv = proj[:, o_ckv:o_kr]
    ckvn = ckv * lax.rsqrt(jnp.mean(ckv * ckv, axis=-1, keepdims=True) + RMS_EPS) * gckv_ref[...]
    ckv_ref[...] = ckvn
    krs = proj[:, o_kr:o_dq]
    kr_ref[...] = krs[:, :MLA_ROPE]

    e128 = e128_ref[...]
    e64 = e64_ref[...]
    up, dn = LANES - MLA_ROPE // 2, MLA_ROPE // 2
    qm = _dot(cqn.astype(BF16), wuq_ref[...])
    for h in range(MLA_HEADS):
        sl = slice(h * LANES, (h + 1) * LANES)
        qm_ref[:, sl] = _norm_rope(qm[:, sl], tab_ref, 0, e128, 1.0 / MLA_QK, up, dn).astype(BF16)

    kv = _dot(ckvn.astype(BF16), wkv_ref[...])
    lane = lax.broadcasted_iota(I32, (1, LANES), 1)
    kr_mid = krs * ((lane >= MLA_NOPE) & (lane < MLA_QK)).astype(F32)
    for h in range(MLA_HEADS):
        sl = slice(h * LANES, (h + 1) * LANES)
        km_ref[:, sl] = _norm_rope(kv[:, sl] + kr_mid, tab_ref, 3, e128, 1.0 / MLA_QK, up, dn).astype(BF16)
    vm_ref[...] = kv[:, MLA_HEADS * LANES:].astype(BF16)

    up, dn = LANES - DIFF_HD // 2, DIFF_HD // 2
    for j in range(DIFF_MAPS * DIFF_HD // LANES):
        sl = slice(j * LANES, (j + 1) * LANES)
        dq_ref[:, sl] = _norm_rope(proj[:, o_dq + j * LANES:o_dq + (j + 1) * LANES], tab_ref, 6, e64,
                                   1.0 / DIFF_HD, up, dn).astype(BF16)
        dk = _norm_rope(proj[:, o_dk + j * LANES:o_dk + (j + 1) * LANES], tab_ref, 9, e64, 1.0 / DIFF_HD, up, dn)
        dk32_ref[:, sl] = dk
        dk16_ref[:, sl] = dk.astype(BF16)
    dv = proj[:, o_dv:]
    dv32_ref[...] = dv
    dv16_ref[...] = dv.astype(BF16)


def _project(x, tab, n_batch, tm, consts):
    t, d = x.shape
    n_i = t // n_batch // tm
    row = lambda b, i: (b * n_i + i, 0)
    fixed = lambda b, i: (0, 0)
    gattn, win, gcq, gckv, wuq, wkv, e128, e64 = consts
    widths = [(KV_LORA, F32), (MLA_ROPE, F32), (DIFF_MAPS * DIFF_HD, F32), (DIFF_HEADS * DIFF_V, F32),
              (MLA_HEADS * LANES, BF16), (MLA_HEADS * LANES, BF16), (MLA_HEADS * MLA_V, BF16),
              (DIFF_MAPS * DIFF_HD, BF16), (DIFF_MAPS * DIFF_HD, BF16), (DIFF_HEADS * DIFF_V, BF16)]
    return pl.pallas_call(
        _proj_kernel,
        grid=(n_batch, n_i),
        in_specs=[pl.BlockSpec((tm, d), row), pl.BlockSpec((tm, N_TAB * LANES), lambda b, i: (i, 0))]
        + [pl.BlockSpec(c.shape, fixed) for c in consts],
        out_specs=[pl.BlockSpec((tm, w), row) for w, _ in widths],
        out_shape=[jax.ShapeDtypeStruct((t, w), dt) for w, dt in widths],
        compiler_params=pltpu.CompilerParams(dimension_semantics=("arbitrary", "arbitrary"),
                                             vmem_limit_bytes=VMEM_LIMIT),
        name="project",
    )(x, tab, *consts)


def _lane_tile(x, width):
    if width <= LANES:
        return x[:, :width]
    return jnp.concatenate([x] * (width // LANES), axis=1)


def _flash_update(a, q, k, v, m_scr, l_scr, acc_scr):
    s = _dot_nt(q, k)
    m_prev = m_scr[a]
    m_next = jnp.maximum(m_prev, jnp.max(s, axis=1, keepdims=True))
    alpha = jnp.exp2(m_prev - m_next)
    p = jnp.exp2(s - _lane_tile(m_next, s.shape[1]))
    l_scr[a] = alpha * l_scr[a] + jnp.sum(p, axis=1, keepdims=True)
    acc_scr[a] = acc_scr[a] * alpha + _dot(p.astype(BF16), v)
    m_scr[a] = m_next


def _pattn_kernel(q_ref, k_ref, v_ref, kmeta_ref, vmeta_ref, gain_ref, lam_ref, o_ref, m_scr, l_scr, acc_scr,
                  *, mla, tq, n_pair, n_meta, post_scale):
    i = pl.program_id(2)
    lane = lax.broadcasted_iota(I32, (1, LANES), 1)
    lo_half = lane < LANES // 2
    slab = lambda j: slice(j * LANES, (j + 1) * LANES)
    maps = range(2 * n_pair)
    if mla:
        qs = [q_ref[:, slab(a)] for a in maps]
        ksl = [slab(a) for a in maps]
    else:
        half = [lo_half.astype(BF16), (1 - lo_half.astype(I32)).astype(BF16)]
        qs = [q_ref[:, slab(a // 2)] * half[a % 2] for a in maps]
        ksl = [slab(a // 2) for a in maps]

    k0 = pl.multiple_of(i * tq, tq)
    causal = lax.broadcasted_iota(I32, (tq, tq), 0) >= lax.broadcasted_iota(I32, (tq, tq), 1)
    meta_ok = lane < n_meta
    for a in maps:
        s1 = jnp.where(causal, _dot_nt(qs[a], k_ref[pl.ds(k0, tq), ksl[a]]), NEG_INF)
        s2 = jnp.where(meta_ok, _dot_nt(qs[a], kmeta_ref[:, ksl[a]]), NEG_INF)
        m = jnp.maximum(jnp.max(s1, axis=1, keepdims=True), jnp.max(s2, axis=1, keepdims=True))
        p1 = jnp.exp2(s1 - m)
        p2 = jnp.exp2(s2 - m)
        m_scr[a] = jnp.broadcast_to(m, (tq, LANES))
        l_scr[a] = jnp.broadcast_to(jnp.sum(p1, axis=1, keepdims=True) + jnp.sum(p2, axis=1, keepdims=True),
                                    (tq, LANES))
        acc_scr[a] = (_dot(p1.astype(BF16), v_ref[pl.ds(k0, tq), slab(a // 2)])
                      + _dot(p2.astype(BF16), vmeta_ref[:, slab(a // 2)]))

    def body(j, carry):
        kj = pl.multiple_of(j * tq, tq)
        for a in maps:
            _flash_update(a, qs[a], k_ref[pl.ds(kj, tq), ksl[a]], v_ref[pl.ds(kj, tq), slab(a // 2)],
                          m_scr, l_scr, acc_scr)
        return carry

    lax.fori_loop(0, i, body, 0)

    for pr in range(n_pair):
        o0 = acc_scr[2 * pr] / l_scr[2 * pr]
        o1 = acc_scr[2 * pr + 1] / l_scr[2 * pr + 1]
        if mla:
            o = jnp.where(lo_half, o0, o1)
            sq = o * o
            ss_lo = jnp.sum(jnp.where(lo_half, sq, 0.0), axis=1, keepdims=True)
            ss_hi = jnp.sum(jnp.where(lo_half, 0.0, sq), axis=1, keepdims=True)
            r = jnp.where(lo_half, lax.rsqrt(ss_lo * (2.0 / LANES) + RMS_EPS),
                          lax.rsqrt(ss_hi * (2.0 / LANES) + RMS_EPS))
        else:
            o = o0 - _lambda(lam_ref) * o1
            r = lax.rsqrt(jnp.mean(o * o, axis=1, keepdims=True) + RMS_EPS)
        o_ref[:, slab(pr)] = (o * r * gain_ref[...] * post_scale).astype(o_ref.dtype)


def _lambda(lam_ref):
    lam = lam_ref[...]
    a = jnp.exp(jnp.sum(lam[0:1] * lam[1:2], axis=1, keepdims=True))
    b = jnp.exp(jnp.sum(lam[2:3] * lam[3:4], axis=1, keepdims=True))
    return a - b + lam[4:5, 0:1]


def _prompt_attn(q, k, v, kmeta, vmeta, gain, lam, n_batch, tq, mla, n_pair, post_scale):
    t = q.shape[0]
    s = t // n_batch
    n_q = s // tq
    n_meta = kmeta.shape[0]
    assert n_meta <= LANES
    kmeta = jnp.pad(kmeta, ((0, LANES - n_meta), (0, 0)))
    vmeta = jnp.pad(vmeta, ((0, LANES - n_meta), (0, 0)))
    kw = (2 * LANES if mla else LANES) * n_pair
    vw = LANES * n_pair
    n_hp = q.shape[1] // kw
    kern = functools.partial(_pattn_kernel, mla=mla, tq=tq, n_pair=n_pair, n_meta=n_meta, post_scale=post_scale)
    return pl.pallas_call(
        kern,
        grid=(n_batch, n_hp, n_q),
        in_specs=[pl.BlockSpec((tq, kw), lambda b, h, i: (b * n_q + i, h)),
                  pl.BlockSpec((s, kw), lambda b, h, i: (b, h)),
                  pl.BlockSpec((s, vw), lambda b, h, i: (b, h)),
                  pl.BlockSpec((LANES, kw), lambda b, h, i: (0, h)),
                  pl.BlockSpec((LANES, vw), lambda b, h, i: (0, h)),
                  pl.BlockSpec((1, LANES), lambda b, h, i: (0, 0)),
                  pl.BlockSpec(lam.shape, lambda b, h, i: (0, 0))],
        out_specs=pl.BlockSpec((tq, vw), lambda b, h, i: (b * n_q + i, h)),
        out_shape=jax.ShapeDtypeStruct((t, n_hp * vw), BF16),
        scratch_shapes=[pltpu.VMEM((2 * n_pair, tq, LANES), F32)] * 3,
        compiler_params=pltpu.CompilerParams(dimension_semantics=("arbitrary",) * 3, vmem_limit_bytes=VMEM_LIMIT),
        name="prompt_attn_mla" if mla else "prompt_attn_diff",
    )(q, k, v, kmeta, vmeta, gain, lam)


Q_ABS_W = 2 * LANES
PAGES_PER_STEP = 32


def _absorb_weights(w_ukv, g_kn_mla):
    wk = w_ukv.reshape(KV_LORA, MLA_HEADS, MLA_NOPE + MLA_V)[..., :MLA_NOPE]
    h = MLA_ROPE // 2
    gn, g1, g2 = g_kn_mla[:MLA_NOPE], g_kn_mla[MLA_NOPE:MLA_NOPE + h], g_kn_mla[MLA_NOPE + h:]
    m = jnp.zeros((MLA_HEADS, LANES, Q_ABS_W), F32)
    m = m.at[:, :MLA_NOPE, :KV_LORA].set(jnp.transpose(wk, (1, 2, 0)) * gn[None, :, None])
    i = jnp.arange(h)
    m = m.at[:, MLA_NOPE + i, LANES + i].set(g1)
    m = m.at[:, MLA_NOPE + h + i, LANES + h + i].set(g2)
    m = m.at[:, MLA_NOPE + h + i, LANES + MLA_ROPE + i].set(g1)
    m = m.at[:, MLA_NOPE + i, LANES + MLA_ROPE + h + i].set(-g2)
    return m.astype(BF16)


def _absorb_kernel(q_ref, m_ref, o_ref):
    n = q_ref.shape[0] // 8
    for h in range(MLA_HEADS):
        o_ref[:, h, :, :] = _dot(q_ref[:, h * LANES:(h + 1) * LANES], m_ref[h]).reshape(n, 8, Q_ABS_W)


def _absorb(qm, m_abs, n_new):
    t = qm.shape[0]
    tm = PAGE
    assert n_new == 8 and t % tm == 0
    return pl.pallas_call(
        _absorb_kernel,
        grid=(t // tm,),
        in_specs=[pl.BlockSpec((tm, MLA_HEADS * LANES), lambda i: (i, 0)),
                  pl.BlockSpec(m_abs.shape, lambda i: (0, 0, 0))],
        out_specs=pl.BlockSpec((tm // n_new, MLA_HEADS, n_new, Q_ABS_W), lambda i: (i, 0, 0, 0)),
        out_shape=jax.ShapeDtypeStruct((t // n_new, MLA_HEADS, n_new, Q_ABS_W), F32),
        compiler_params=pltpu.CompilerParams(dimension_semantics=("arbitrary",), vmem_limit_bytes=VMEM_LIMIT),
        name="absorb_q",
    )(qm, m_abs)


def _softmax_step(s, m_scr, l_scr, acc_scr):
    m_prev = m_scr[...]
    m_next = jnp.maximum(m_prev, jnp.max(s, axis=1, keepdims=True))
    alpha = jnp.exp2(m_prev - m_next)
    w = s.shape[1]
    p = jnp.exp2(s - _lane_tile(m_next, w))
    l_scr[...] = alpha * l_scr[...] + jnp.sum(p, axis=1, keepdims=True)
    acc_scr[...] = acc_scr[...] * alpha
    m_scr[...] = m_next
    return p


def _decode_kernel(pt_ref, qabs_ref, qrow_ref, knew_ref, vnew_ref, qd_ref, kdnew_ref, vdnew_ref, tabc_ref, tabs_ref,
                   wukt_ref, wuv_ref, go_ref, gsub_ref, lam_ref, ckv_hbm, kr_hbm, dk_hbm, dv_hbm, oa_ref, ob_ref,
                   ma, la, acca, md, ld, accd, sa, sd, ckv_buf, kr_buf, dk_buf, dv_buf, sems, *, n_pg, post_scale):
    c = pl.program_id(1)
    n_chunks = pl.num_programs(1)
    step = pl.program_id(0) * n_chunks + c
    last_step = pl.num_programs(0) * n_chunks - 1
    slot = step % 2
    other = 1 - slot
    nxt = jnp.minimum(step + 1, last_step)
    n_rows = MLA_HEADS * 8
    pools = ((ckv_hbm, ckv_buf), (kr_hbm, kr_buf), (dk_hbm, dk_buf), (dv_hbm, dv_buf))

    def fetch_page(step_idx, buf_slot, p):
        page = pt_ref[step_idx * n_pg + p]
        for k, (pool, buf) in enumerate(pools):
            pltpu.make_async_copy(pool.at[page], buf.at[buf_slot, p], sems.at[k, buf_slot]).start()

    def wait_pages(buf_slot):
        for k, (pool, buf) in enumerate(pools):
            pltpu.make_async_copy(pool.at[pl.ds(0, n_pg)], buf.at[buf_slot], sems.at[k, buf_slot]).wait()

    @pl.when(step == 0)
    def _():
        for p in range(n_pg):
            fetch_page(0, 0, p)

    @pl.when(c == 0)
    def _():
        for m_scr, l_scr, acc_scr in ((ma, la, acca), (md, ld, accd)):
            m_scr[...] = jnp.full(m_scr.shape, NEG_INF, F32)
            l_scr[...] = jnp.zeros(l_scr.shape, F32)
            acc_scr[...] = jnp.zeros(acc_scr.shape, F32)

    wait_pages(slot)
    qabs = qabs_ref[...].reshape(n_rows, Q_ABS_W)
    lhs_lat = jnp.concatenate([qabs[:, :KV_LORA], wukt_ref[...]], axis=0)
    q_cs = qabs[:, LANES:LANES + 2 * MLA_ROPE]
    qd = qd_ref[...]
    ri = lax.broadcasted_iota(I32, (n_rows, DIFF_MAPS * DIFF_HD), 0)
    ci = lax.broadcasted_iota(I32, (n_rows, DIFF_MAPS * DIFF_HD), 1)
    q_bd = jnp.where(ri // 8 == ci // DIFF_HD, jnp.concatenate([qd] * DIFF_MAPS, axis=1), 0.0)

    for p in range(n_pg):
        cols = slice(p * PAGE, (p + 1) * PAGE)
        fetch_page(nxt, other, p)
        kr_t = kr_buf[slot, p]
        big = _dot_nt(lhs_lat, ckv_buf[slot, p])
        kn_t = big[n_rows:]
        ss = jnp.sum((kn_t * kn_t).reshape(MLA_HEADS, MLA_NOPE, PAGE), axis=1)
        ss = ss + jnp.sum(kr_t * kr_t, axis=0, keepdims=True)
        r_t = lax.rsqrt(ss * (1.0 / MLA_QK) + RMS_EPS)
        f_cs = jnp.concatenate([kr_t * tabc_ref[:, cols], kr_t * tabs_ref[:, cols]], axis=0)
        s_pg = big[:n_rows] + _dot(q_cs, f_cs)
        for h in range(MLA_HEADS):
            sa[h * 8:(h + 1) * 8, cols] = s_pg[h * 8:(h + 1) * 8, :] * r_t[h:h + 1, :]
        sd[:, cols] = _dot(q_bd, dk_buf[slot, p].reshape(DIFF_MAPS * DIFF_HD, PAGE))

    pa = _softmax_step(sa[...], ma, la, acca)
    pd = _softmax_step(sd[...], md, ld, accd)
    acc_a = acca[...]
    acc_d = accd[...]
    head_of_row = lax.broadcasted_iota(I32, (n_rows, PAGE), 0) // 16
    for p in range(n_pg):
        cols = slice(p * PAGE, (p + 1) * PAGE)
        acc_a = acc_a + _dot(pa[:, cols], ckv_buf[slot, p])
        v_stack = jnp.concatenate([dv_buf[slot, p, pl.ds(h, PAGE, stride=DIFF_HEADS), :]
                                   for h in range(DIFF_HEADS)], axis=0)
        pd_pg = pd[:, cols]
        p_wide = jnp.concatenate([jnp.where(head_of_row == h, pd_pg, 0.0) for h in range(DIFF_HEADS)], axis=1)
        acc_d = acc_d + _dot(p_wide, v_stack)
    acca[...] = acc_a
    accd[...] = acc_d

    @pl.when(step == last_step)
    def _():
        wait_pages(other)

    @pl.when(c == pl.num_programs(1) - 1)
    def _():
        ri = lax.broadcasted_iota(I32, (n_rows, n_rows), 0)
        ci = lax.broadcasted_iota(I32, (n_rows, n_rows), 1)
        valid = (ri // 8 == ci // 8) & (ci % 8 <= ri % 8)

        s_new = jnp.where(valid, _dot_nt(qrow_ref[...], knew_ref[...]), NEG_INF)
        p_new = _softmax_step(s_new, ma, la, acca)
        o_new = _dot(p_new, vnew_ref[...])
        lat = acca[...]
        lat_hi = lat.astype(BF16).astype(F32)
        lat_lo = lat - lat_hi
        l_a = la[...]
        for h in range(MLA_HEADS):
            rows = slice(h * 8, (h + 1) * 8)
            wv = wuv_ref[h]
            o = (_dot(lat_hi[rows], wv) + _dot(lat_lo[rows], wv) + o_new[rows]) / l_a[rows, :MLA_V]
            r = lax.rsqrt(jnp.mean(o * o, axis=1, keepdims=True) + RMS_EPS)
            oa_ref[rows, :] = (o * r * go_ref[...]).astype(oa_ref.dtype)

        sd_new = jnp.where(valid, _dot_nt(qd, kdnew_ref[...]), NEG_INF)
        pd_new = _softmax_step(sd_new, md, ld, accd)
        od = (accd[...] + _dot(pd_new, vdnew_ref[...])) / ld[...]
        lam = _lambda(lam_ref)
        for h in range(DIFF_HEADS):
            o = od[2 * h * 8:(2 * h + 1) * 8] - lam * od[(2 * h + 1) * 8:(2 * h + 2) * 8]
            r = lax.rsqrt(jnp.mean(o * o, axis=1, keepdims=True) + RMS_EPS)
            ob_ref[h * 8:(h + 1) * 8, :] = (o * r * gsub_ref[...] * post_scale).astype(ob_ref.dtype)


def _decode_attn(new, pool_ckv, pool_kr, pool_dk, pool_dv, page_table, w_ukv, gains, g_o, g_subln, lam, lam_init,
                 n_seq, n_new):
    assert n_new == 8
    n_pages = page_table.shape[1]
    n_pg = min(PAGES_PER_STEP, n_pages)
    assert n_pages % n_pg == 0
    past_len = n_pages * PAGE
    qm, km, vm, dq, dk16, dv16 = new[4:10]

    def rows_by_head(x, n_h):
        w = x.shape[1] // n_h
        x = jnp.transpose(x.reshape(n_seq, n_new, n_h, w), (0, 2, 1, 3))
        return x.reshape(n_seq, n_h * n_new, w).astype(F32)

    qabs = _absorb(qm, _absorb_weights(w_ukv, gains[1]), n_new)
    qrow = rows_by_head(qm, MLA_HEADS)
    knew = rows_by_head(km, MLA_HEADS)
    vnew = rows_by_head(vm, MLA_HEADS)
    qd = rows_by_head(dq, DIFF_MAPS)
    kdnew = rows_by_head(dk16, DIFF_MAPS)
    vdnew = jnp.repeat(rows_by_head(dv16, DIFF_HEADS).reshape(n_seq, DIFF_HEADS, n_new, DIFF_V), 2, axis=1)
    vdnew = vdnew.reshape(n_seq, DIFF_MAPS * n_new, DIFF_V)

    pool_kr = jnp.transpose(pool_kr, (0, 2, 1))
    pool_dk = jnp.transpose(pool_dk, (0, 2, 3, 1))
    pool_dv = pool_dv.reshape(pool_dv.shape[0], PAGE * DIFF_HEADS, DIFF_V)
    cs, sn = _rope_cs(jnp.arange(past_len), MLA_ROPE)
    tabc = jnp.concatenate([cs, cs], axis=1).T
    tabs = jnp.concatenate([sn, sn], axis=1).T
    wkv = w_ukv.reshape(KV_LORA, MLA_HEADS, MLA_NOPE + MLA_V)
    wukt = wkv[..., :MLA_NOPE].reshape(KV_LORA, MLA_HEADS * MLA_NOPE).T
    wuv = jnp.transpose(wkv[..., MLA_NOPE:], (1, 0, 2))

    per_seq = lambda a: pl.BlockSpec((None,) + a.shape[1:], lambda s, c, pt: (s,) + (0,) * (a.ndim - 1))
    fixed = lambda a: pl.BlockSpec(a.shape, lambda s, c, pt: (0,) * a.ndim)
    tab_spec = pl.BlockSpec((MLA_ROPE, n_pg * PAGE), lambda s, c, pt: (0, c))

    seq_in = (qabs, qrow, knew, vnew, qd, kdnew, vdnew)
    consts = (wukt, wuv, g_o[:, :MLA_V], g_subln, lam)
    pools = (pool_ckv, pool_kr, pool_dk, pool_dv)
    n_rows = MLA_HEADS * n_new
    grid_spec = pltpu.PrefetchScalarGridSpec(
        num_scalar_prefetch=1,
        grid=(n_seq, n_pages // n_pg),
        in_specs=[per_seq(a) for a in seq_in] + [tab_spec, tab_spec] + [fixed(a) for a in consts]
        + [pl.BlockSpec(memory_space=pl.ANY)] * len(pools),
        out_specs=[pl.BlockSpec((None, n_rows, MLA_V), lambda s, c, pt: (s, 0, 0)),
                   pl.BlockSpec((None, DIFF_HEADS * n_new, DIFF_V), lambda s, c, pt: (s, 0, 0))],
        scratch_shapes=[pltpu.VMEM((n_rows, LANES), F32)] * 6
        + [pltpu.VMEM((n_rows, n_pg * PAGE), F32)] * 2
        + [pltpu.VMEM((2, n_pg) + pool.shape[1:], F32) for pool in pools]
        + [pltpu.SemaphoreType.DMA((len(pools), 2))],
    )
    oa, ob = pl.pallas_call(
        functools.partial(_decode_kernel, n_pg=n_pg, post_scale=1.0 - lam_init),
        grid_spec=grid_spec,
        out_shape=[jax.ShapeDtypeStruct((n_seq, n_rows, MLA_V), BF16),
                   jax.ShapeDtypeStruct((n_seq, DIFF_HEADS * n_new, DIFF_V), BF16)],
        compiler_params=pltpu.CompilerParams(dimension_semantics=("arbitrary", "arbitrary"),
                                             vmem_limit_bytes=VMEM_LIMIT),
        name="decode_attn",
    )(page_table.reshape(-1), *seq_in, tabc, tabs, *consts, *pools)

    def tokens_major(o, n_h):
        w = o.shape[2]
        return jnp.transpose(o.reshape(n_seq, n_h, n_new, w), (0, 2, 1, 3)).reshape(n_seq * n_new, n_h * w)

    return tokens_major(oa, MLA_HEADS), tokens_major(ob, DIFF_HEADS)


def _split_bf16(x):
    hi = x.astype(BF16)
    return hi, (x - hi.astype(F32)).astype(BF16)


def _out_gate_kernel(x_ref, oa_ref, ob_ref, woa_ref, wob_ref, gffn_ref, wg_hi_ref, wg_lo_ref, bias_ref,
                     hp_ref, gsel_ref, comb_ref):
    hp = x_ref[...] + _dot(oa_ref[...], woa_ref[...]) + _dot(ob_ref[...], wob_ref[...])
    hp_ref[...] = hp
    h = hp * lax.rsqrt(jnp.mean(hp * hp, axis=-1, keepdims=True) + RMS_EPS) * gffn_ref[...]
    h_hi, h_lo = _split_bf16(h)
    w_hi = wg_hi_ref[...]
    lt = _dot_nt(w_hi, h_hi) + _dot_nt(w_hi, h_lo) + _dot_nt(wg_lo_ref[...], h_hi) + bias_ref[...]

    gl = [lt[g:g + 1, :] for g in range(N_GROUPS)]
    gmax = functools.reduce(jnp.maximum, gl)
    gsel = jnp.full(gmax.shape, N_GROUPS - 1, I32)
    for g in range(N_GROUPS - 2, -1, -1):
        gsel = jnp.where(gl[g] == gmax, g, gsel)
    gw = 1.0 / functools.reduce(lambda a, b: a + b, [jnp.exp(x - gmax) for x in gl])

    base = 8
    el = []
    for e in range(EXP_PER_GROUP):
        v = lt[base + (N_GROUPS - 1) * EXP_PER_GROUP + e:base + (N_GROUPS - 1) * EXP_PER_GROUP + e + 1, :]
        for g in range(N_GROUPS - 2, -1, -1):
            v = jnp.where(gsel == g, lt[base + g * EXP_PER_GROUP + e:base + g * EXP_PER_GROUP + e + 1, :], v)
        el.append(v)
    v1 = functools.reduce(jnp.maximum, el)
    i1 = jnp.full(v1.shape, EXP_PER_GROUP - 1, I32)
    for e in range(EXP_PER_GROUP - 2, -1, -1):
        i1 = jnp.where(el[e] == v1, e, i1)
    rest = [jnp.where(i1 == e, -jnp.inf, el[e]) for e in range(EXP_PER_GROUP)]
    v2 = functools.reduce(jnp.maximum, rest)
    i2 = jnp.full(v1.shape, EXP_PER_GROUP - 1, I32)
    for e in range(EXP_PER_GROUP - 2, -1, -1):
        i2 = jnp.where(rest[e] == v2, e, i2)
    e2 = jnp.exp(v2 - v1)
    w1 = gw / (1.0 + e2)
    w2 = gw * e2 / (1.0 + e2)
    gsel_ref[...] = gsel
    for e in range(EXP_PER_GROUP):
        comb_ref[e:e + 1, :] = jnp.where(i1 == e, w1, jnp.where(i2 == e, w2, 0.0))


def _out_gate_both_kernel(xp_ref, oap_ref, obp_ref, xs_ref, oas_ref, obs_ref, *rest, n_prompt_tiles):
    i = pl.program_id(0)

    @pl.when(i < n_prompt_tiles)
    def _():
        _out_gate_kernel(xp_ref, oap_ref, obp_ref, *rest)

    @pl.when(i >= n_prompt_tiles)
    def _():
        _out_gate_kernel(xs_ref, oas_ref, obs_ref, *rest)


def _out_gate(prompt, sample, consts):
    (xp, oap, obp), (xs, oas, obs) = prompt, sample
    (n_p, d), n_s = xp.shape, xs.shape[0]
    tm = math.gcd(256, n_s)
    assert n_p % tm == 0
    npt, n_t = n_p // tm, (n_p + n_s) // tm
    p_row = lambda i: (jnp.minimum(i, npt - 1), 0)
    s_row = lambda i: (jnp.maximum(i - npt, 0), 0)
    fixed = lambda i: (0, 0)
    specs = lambda arrs, idx: [pl.BlockSpec((tm, a.shape[1]), idx) for a in arrs]
    return pl.pallas_call(
        functools.partial(_out_gate_both_kernel, n_prompt_tiles=npt),
        grid=(n_t,),
        in_specs=specs(prompt, p_row) + specs(sample, s_row) + [pl.BlockSpec(c.shape, fixed) for c in consts],
        out_specs=[pl.BlockSpec((tm, d), lambda i: (i, 0)), pl.BlockSpec((1, tm), lambda i: (0, i)),
                   pl.BlockSpec((EXP_PER_GROUP, tm), lambda i: (0, i))],
        out_shape=[jax.ShapeDtypeStruct((n_p + n_s, d), F32), jax.ShapeDtypeStruct((1, n_p + n_s), I32),
                   jax.ShapeDtypeStruct((EXP_PER_GROUP, n_p + n_s), F32)],
        compiler_params=pltpu.CompilerParams(dimension_semantics=("arbitrary",), vmem_limit_bytes=VMEM_LIMIT),
        name="out_gate",
    )(xp, oap, obp, xs, oas, obs, *consts)


def _moe_kernel(tg_ref, src_ref, dst_ref, hp_ref, comb_ref, gffn_ref, w1_ref, w3_ref, w2_ref,
                y_ref, xbuf, obuf, sem_in, sem_out, *, tmoe):
    t = pl.program_id(0)
    n_t = pl.num_programs(0)
    slot = t % 2
    other = 1 - slot
    nxt = jnp.minimum(t + 1, n_t - 1)
    chunk = tmoe // EXP_PER_GROUP

    def gather(tile, buf_slot, r, queue=0):
        row = src_ref[tile * tmoe + r]
        pltpu.make_async_copy(hp_ref.at[pl.ds(row, 1)], xbuf.at[buf_slot, pl.ds(r, 1)], sem_in.at[buf_slot]).start(
            priority=queue)

    def scatter(tile_shifted, buf_slot, r, queue=0):
        row = dst_ref[tile_shifted * tmoe + r]
        pltpu.make_async_copy(obuf.at[buf_slot, pl.ds(r, 1)], y_ref.at[pl.ds(row, 1)], sem_out.at[buf_slot]).start(
            priority=queue)

    def wait_gathers(buf_slot):
        pltpu.make_async_copy(hp_ref.at[pl.ds(0, tmoe)], xbuf.at[buf_slot], sem_in.at[buf_slot]).wait()

    def wait_scatters(buf_slot):
        pltpu.make_async_copy(obuf.at[buf_slot], y_ref.at[pl.ds(0, tmoe)], sem_out.at[buf_slot]).wait()

    @pl.when(t == 0)
    def _():
        obuf[...] = jnp.zeros_like(obuf)

        def first(r, c):
            gather(0, 0, r)
            return c

        lax.fori_loop(0, tmoe, first, 0, unroll=8)

    wait_gathers(slot)
    x = xbuf[slot]
    h = (x * lax.rsqrt(jnp.mean(x * x, axis=-1, keepdims=True) + RMS_EPS) * gffn_ref[...]).astype(BF16)
    comb = comb_ref[...]
    obuf[slot] = x
    for e in range(EXP_PER_GROUP):
        for r in range(e * chunk, (e + 1) * chunk):
            gather(nxt, other, r, r % 2)
            scatter(t, other, r, (r + 1) % 2)
        a = _dot(h, w1_ref[e])
        b = _dot(h, w3_ref[e])
        hid = a * (1.0 / (1.0 + jnp.exp(-a))) * b * comb[:, e:e + 1]
        obuf[slot] += _dot(hid.astype(BF16), w2_ref[e])
    wait_scatters(other)

    @pl.when(t == n_t - 1)
    def _():
        def last(r, c):
            scatter(t + 1, slot, r)
            return c

        lax.fori_loop(0, tmoe, last, 0, unroll=8)
        wait_scatters(slot)
        wait_gathers(other)


def _moe(hp_all, n_all, gsel, comb, g_ffn, w1, w3, w2, tmoe):
    d = hp_all.shape[1]
    n_tiles = n_all // tmoe + N_GROUPS
    onehot = (gsel[:, None] == jnp.arange(N_GROUPS, dtype=I32)[None, :]).astype(I32)
    csum = jnp.cumsum(onehot, axis=0)
    rank = jnp.take_along_axis(csum, gsel[:, None], axis=1)[:, 0] - 1
    g_tiles = (csum[-1] + tmoe - 1) // tmoe
    g_first = jnp.cumsum(g_tiles) - g_tiles
    dest = g_first[gsel] * tmoe + rank
    row_tok = jnp.full((n_tiles * tmoe,), -1, I32).at[dest].set(jnp.arange(n_all, dtype=I32))
    tile_group = jnp.sum(jnp.arange(n_tiles, dtype=I32)[:, None] >= g_first[None, 1:], axis=1).astype(I32)
    comb_sorted = jnp.where(row_tok[:, None] >= 0, comb.T[jnp.maximum(row_tok, 0)], 0.0)
    spare = n_all + jnp.arange(tmoe, dtype=I32)
    src_row = jnp.maximum(row_tok, 0)
    dst_row = jnp.where(row_tok >= 0, row_tok, jnp.tile(spare, n_tiles))
    dst_row = jnp.concatenate([spare, dst_row])

    w1g, w3g, w2g = w1.astype(BF16), w3.astype(BF16), w2.astype(BF16)
    group_w = lambda w: pl.BlockSpec((None,) + w.shape[1:], lambda t, tg, src, dst: (tg[t], 0, 0, 0))

    grid_spec = pltpu.PrefetchScalarGridSpec(
        num_scalar_prefetch=3,
        grid=(n_tiles,),
        in_specs=[pl.BlockSpec(memory_space=pl.ANY),
                  pl.BlockSpec((tmoe, EXP_PER_GROUP), lambda t, tg, src, dst: (t, 0)),
                  pl.BlockSpec((1, d), lambda t, tg, src, dst: (0, 0)),
                  group_w(w1g), group_w(w3g), group_w(w2g)],
        out_specs=pl.BlockSpec(memory_space=pl.ANY),
        scratch_shapes=[pltpu.VMEM((2, tmoe, d), F32), pltpu.VMEM((2, tmoe, d), F32),
                        pltpu.SemaphoreType.DMA((2,)), pltpu.SemaphoreType.DMA((2,))],
    )
    return pl.pallas_call(
        functools.partial(_moe_kernel, tmoe=tmoe),
        grid_spec=grid_spec,
        out_shape=jax.ShapeDtypeStruct((n_all + tmoe, d), F32),
        compiler_params=pltpu.CompilerParams(dimension_semantics=("arbitrary",), vmem_limit_bytes=VMEM_LIMIT),
        name="moe",
    )(tile_group, src_row, dst_row, hp_all, comb_sorted, g_ffn, w1g, w3g, w2g)


def kernel(x_prompt, x_sample, cache_ckv, cache_krope, cache_dk, cache_dv, page_table, meta_tokens, g_attn, w_in, g_cq, g_ckv, w_uq, g_qn_mla, w_ukv, g_kn_mla, g_o_mla, g_qn_diff, g_kn_diff, lambda_q1, lambda_k1, lambda_q2, lambda_k2, g_subln, w_o, g_ffn, w_gate_group, b_gate_group, w_gate_expert, b_gate_expert, w1, w3, w2):
    assert g_attn.shape[0] == 1, "single layer step"
    n_b, seq, d = x_prompt.shape
    n_seq, n_new, _ = x_sample.shape
    n_meta = meta_tokens.shape[0]
    past_len = page_table.shape[1] * PAGE
    lam_init = 0.8 - 0.6 * math.exp(-0.3 * 0)

    w_in_p, wq_p, wkv_p = _prep_proj_weights(w_in[0], w_uq[0], w_ukv[0])
    pconsts = (g_attn, w_in_p, g_cq, g_ckv, wq_p, wkv_p, _seg_ones(LANES), _seg_ones(DIFF_HD))
    gains = (g_qn_mla[0], g_kn_mla[0], g_qn_diff[0], g_kn_diff[0])
    tab_meta = _tables(jnp.arange(n_meta), *gains)
    tab_prompt = _tables(n_meta + jnp.arange(seq), *gains)
    tab_new = jnp.tile(_tables(past_len + jnp.arange(n_new), *gains), (PAGE // n_new, 1))

    tm = min(256, seq)
    meta = _project(meta_tokens.astype(F32), tab_meta, 1, n_meta, pconsts)
    main = _project(x_prompt.reshape(n_b * seq, d), tab_prompt, n_b, tm, pconsts)
    new = _project(x_sample.reshape(n_seq * n_new, d), tab_new, (n_seq * n_new) // PAGE, PAGE, pconsts)

    lam = jnp.concatenate([lambda_q1, lambda_k1, lambda_q2, lambda_k2, jnp.full((1, DIFF_HD), lam_init, F32)], axis=0)
    g_o = jnp.tile(g_o_mla, (1, LANES // MLA_V))
    tq = min(512, seq)
    oa_p = _prompt_attn(main[4], main[5], main[6], meta[5], meta[6], g_o, lam, n_b, tq, True, ATTN_PAIRS, 1.0)
    ob_p = _prompt_attn(main[7], main[8], main[9], meta[8], meta[9], g_subln, lam, n_b, tq, False, ATTN_PAIRS,
                        1.0 - lam_init)

    oa_s, ob_s = _decode_attn(new, cache_ckv[0], cache_krope[0], cache_dk[0], cache_dv[0], page_table,
                              w_ukv[0], gains, g_o, g_subln, lam, lam_init, n_seq, n_new)

    wo = w_o[0].astype(BF16)
    n_a = MLA_HEADS * MLA_V
    wg = jnp.concatenate([w_gate_group[0].T, jnp.zeros((8 - N_GROUPS, d), F32),
                          w_gate_expert[0].reshape(d, N_GROUPS * EXP_PER_GROUP).T], axis=0)
    wg_hi = wg.astype(BF16)
    wg_lo = (wg - wg_hi.astype(F32)).astype(BF16)
    bias = jnp.concatenate([b_gate_group[0], jnp.zeros((8 - N_GROUPS,), F32), b_gate_expert[0].reshape(-1)])[:, None]
    gconsts = (wo[:n_a], wo[n_a:], g_ffn, wg_hi, wg_lo, bias)
    n_p, n_s = n_b * seq, n_seq * n_new
    n_all = n_p + n_s
    hp_all, gsel, comb = _out_gate((x_prompt.reshape(n_p, d), oa_p, ob_p), (x_sample.reshape(n_s, d), oa_s, ob_s),
                                   gconsts)
    y_all = _moe(hp_all, n_all, gsel[0], comb, g_ffn, w1[0], w3[0], w2[0], 256)
    y_p, y_s = y_all[:n_p], y_all[n_p:n_all]

    def with_meta(m, x, tail):
        m = jnp.broadcast_to(m[None], (n_b,) + m.shape)
        return jnp.concatenate([m, x.reshape((n_b, seq) + m.shape[2:])], axis=1).reshape((1, n_b, n_meta + seq) + tail)

    return (y_p.reshape(n_b, seq, d), y_s.reshape(n_seq, n_new, d),
            with_meta(meta[0], main[0], (KV_LORA,)), with_meta(meta[1], main[1], (MLA_ROPE,)),
            with_meta(meta[2], main[2], (DIFF_MAPS, DIFF_HD)), with_meta(meta[3], main[3], (DIFF_HEADS, DIFF_V)),
            new[0].reshape(1, n_seq, n_new, KV_LORA), new[1].reshape(1, n_seq, n_new, MLA_ROPE),
            new[2].reshape(1, n_seq, n_new, DIFF_MAPS, DIFF_HD), new[3].reshape(1, n_seq, n_new, DIFF_HEADS, DIFF_V))
```
